```python
import math
import jax, jax.numpy as jnp
from jax import lax
import numpy as np

D_MODEL = 2048
BATCH = 4
SEQ = 4096
DEPTH = 2

N_MIXERS = 2
N_CONV_LAYERS = (DEPTH + 1) // 2
N_ATTN_LAYERS = DEPTH // 2

CONV_WIDTH = 31

HEAD_DIM = 128
N_HEADS = D_MODEL // HEAD_DIM
N_KV_HEADS = 4
GROUP = N_HEADS // N_KV_HEADS
IDX_HEADS = 16
IDX_DIM = 64
INDEX_TOPK_MAX = 256
QUERY_BLOCK = 128
Q_COLS = N_HEADS * HEAD_DIM
KV_COLS = N_KV_HEADS * HEAD_DIM
QI_COLS = IDX_HEADS * IDX_DIM
IN_COLS = Q_COLS + 2 * KV_COLS + QI_COLS + IDX_DIM + IDX_HEADS

NUM_BUCKETS = 32
MAX_DISTANCE = 128

N_EXPERTS = 32
TOP_K = 4
D_FF = D_MODEL
SWIGLU_LIMIT = 7.0
SWIGLU_ALPHA = 1.702
MOE_BLOCK = 256

NORM_EPS = 1e-5

kernel_name = "hybrid_conformer_conv_dsa_moe"


def rmsnorm(x, g):
    xf = x.astype(jnp.float32)
    y = xf * lax.rsqrt(jnp.mean(xf * xf, axis=-1, keepdims=True) + NORM_EPS)
    return (y * g.astype(jnp.float32)).astype(x.dtype)


def layernorm(x, g, b):
    xf = x.astype(jnp.float32)
    mu = jnp.mean(xf, axis=-1, keepdims=True)
    var = jnp.mean(jnp.square(xf - mu), axis=-1, keepdims=True)
    y = (xf - mu) * lax.rsqrt(var + NORM_EPS)
    return (y * g.astype(jnp.float32) + b.astype(jnp.float32)).astype(x.dtype)


def conformer_conv(h, w_pw1, b_pw1, w_dw, b_dw, ln_g, ln_b, w_pw2, b_pw2):
    u = h @ w_pw1 + b_pw1
    a, gate = jnp.split(u, 2, axis=-1)
    u = a * jax.nn.sigmoid(gate)
    u = lax.conv_general_dilated(
        u, w_dw, window_strides=(1,), padding=[(CONV_WIDTH - 1, 0)],
        dimension_numbers=("NWC", "WIO", "NWC"), feature_group_count=u.shape[-1],
    ) + b_dw
    u = layernorm(u, ln_g, ln_b)
    u = jax.nn.silu(u)
    return u @ w_pw2 + b_pw2


def t5_bucket(n):
    n = jnp.maximum(n, 0)
    max_exact = NUM_BUCKETS // 2
    nf = jnp.maximum(n, 1).astype(jnp.float32)
    large = max_exact + (jnp.log(nf / max_exact) / math.log(MAX_DISTANCE / max_exact)
                         * (NUM_BUCKETS - max_exact)).astype(jnp.int32)
    large = jnp.minimum(large, NUM_BUCKETS - 1)
    return jnp.where(n < max_exact, n, large)


def dsa_attention(h, w_in, w_o, rel_bias):
    b, s, _ = h.shape
    proj = h @ w_in
    cuts = np.cumsum([Q_COLS, KV_COLS, KV_COLS, QI_COLS, IDX_DIM]).tolist()
    q, k, v, qi, ki, wi = jnp.split(proj, cuts, axis=-1)
    q = q.reshape(b, s, N_KV_HEADS, GROUP, HEAD_DIM)
    k = k.reshape(b, s, N_KV_HEADS, HEAD_DIM)
    v = v.reshape(b, s, N_KV_HEADS, HEAD_DIM)
    qi = qi.reshape(b, s, IDX_HEADS, IDX_DIM)
    wi = wi * (IDX_HEADS ** -0.5)
    topk = min(INDEX_TOPK_MAX, s // 4)
    n_qb = s // QUERY_BLOCK
    kpos = jnp.arange(s, dtype=jnp.int32)
    idx_scale = IDX_DIM ** -0.5
    att_scale = HEAD_DIM ** -0.5
    gather_rows = jax.vmap(lambda arr, ids: arr[ids])

    def block(qb):
        q0 = qb * QUERY_BLOCK
        qpos = q0 + jnp.arange(QUERY_BLOCK, dtype=jnp.int32)
        qi_b = lax.dynamic_slice_in_dim(qi, q0, QUERY_BLOCK, axis=1)
        wi_b = lax.dynamic_slice_in_dim(wi, q0, QUERY_BLOCK, axis=1)
        q_b = lax.dynamic_slice_in_dim(q, q0, QUERY_BLOCK, axis=1)
        hs = jnp.einsum("bqhd,bsd->bqhs", qi_b, ki).astype(jnp.float32) * idx_scale
        isc = jnp.einsum("bqhs,bqh->bqs", jax.nn.relu(hs), wi_b.astype(jnp.float32))
        causal = kpos[None, :] <= qpos[:, None]
        isc = jnp.where(causal[None], isc, -jnp.inf)
        _, sel = lax.top_k(isc, topk)
        valid = sel <= qpos[None, :, None]
        k_sel = gather_rows(k, sel)
        v_sel = gather_rows(v, sel)
        logits = jnp.einsum("bqngd,bqknd->bqngk", q_b, k_sel).astype(jnp.float32) * att_scale
        bucket = t5_bucket(qpos[None, :, None] - sel)
        bias = rel_bias[bucket].astype(jnp.float32)
        bias = bias.reshape(b, QUERY_BLOCK, topk, N_KV_HEADS, GROUP).transpose(0, 1, 3, 4, 2)
        logits = jnp.where(valid[:, :, None, None, :], logits + bias, -jnp.inf)
        p = jax.nn.softmax(logits, axis=-1).astype(v.dtype)
        o = jnp.einsum("bqngk,bqknd->bqngd", p, v_sel)
        return o.reshape(b, QUERY_BLOCK, Q_COLS)

    outs = lax.map(block, jnp.arange(n_qb, dtype=jnp.int32))
    o = outs.transpose(1, 0, 2, 3).reshape(b, s, Q_COLS)
    return o @ w_o


def moe_ffn(h, router_w, router_b, w_gu, b_gu, w_down, b_down):
    b, s, d = h.shape
    t = b * s
    hf = h.reshape(t, d)
    logits = (hf @ router_w).astype(jnp.float32) + router_b.astype(jnp.float32)
    top_vals, top_idx = lax.top_k(logits, TOP_K)
    gates = jax.nn.softmax(top_vals, axis=-1).astype(h.dtype)
    n = t * TOP_K
    expert_flat = top_idx.reshape(n).astype(jnp.int32)
    token_flat = jnp.repeat(jnp.arange(t, dtype=jnp.int32), TOP_K)
    gate_flat = gates.reshape(n)
    order = jnp.argsort(expert_flat)
    sorted_expert = expert_flat[order]
    counts = jnp.bincount(expert_flat, length=N_EXPERTS).astype(jnp.int32)
    starts = jnp.cumsum(counts) - counts
    padded = ((counts + MOE_BLOCK - 1) // MOE_BLOCK) * MOE_BLOCK
    cum_pad = jnp.cumsum(padded)
    pad_starts = cum_pad - padded
    rank = jnp.arange(n, dtype=jnp.int32) - starts[sorted_expert]
    dest = pad_starts[sorted_expert] + rank
    n_blocks = -(-n // MOE_BLOCK) + N_EXPERTS
    p_rows = n_blocks * MOE_BLOCK
    row_token = jnp.full((p_rows,), t, dtype=jnp.int32).at[dest].set(token_flat[order])
    row_gate = jnp.zeros((p_rows,), h.dtype).at[dest].set(gate_flat[order])
    block_start = jnp.arange(n_blocks, dtype=jnp.int32) * MOE_BLOCK
    block_expert = jnp.minimum(jnp.searchsorted(cum_pad, block_start, side="right"),
                               N_EXPERTS - 1).astype(jnp.int32)
    x_pad = jnp.concatenate([hf, jnp.zeros((1, d), hf.dtype)], axis=0)
    xs = x_pad[row_token].reshape(n_blocks, MOE_BLOCK, d)

    def expert_block(args):
        xb, e = args
        gu = xb @ w_gu[e] + b_gu[e]
        g, u = jnp.split(gu, 2, axis=-1)
        g = jnp.minimum(g, SWIGLU_LIMIT)
        u = jnp.clip(u, -SWIGLU_LIMIT, SWIGLU_LIMIT)
        act = (u + 1.0) * (g * jax.nn.sigmoid(g * SWIGLU_ALPHA))
        return act @ w_down[e] + b_down[e]

    ys = lax.map(expert_block, (xs, block_expert))
    yw = ys.reshape(p_rows, d) * row_gate[:, None]
    out = jax.ops.segment_sum(yw, row_token, num_segments=t + 1)[:t]
    return out.reshape(b, s, d)


def setup_inputs(seed: int = 0) -> dict:
    key = jax.random.key(seed)
    ks = jax.random.split(key, 32)
    f32 = jnp.float32
    D = D_MODEL
    nrm = lambda k, shape, sc: (jax.random.normal(k, shape, f32) * sc).astype(f32)
    return {
        "x": nrm(ks[0], (BATCH, SEQ, D), 1.0),
        "norm_mix": 1.0 + nrm(ks[1], (DEPTH, D), 0.02),
        "norm_ffn": 1.0 + nrm(ks[2], (DEPTH, D), 0.02),
        "final_norm": 1.0 + nrm(ks[3], (D,), 0.02),
        "conv_w_pw1": nrm(ks[4], (N_CONV_LAYERS, D, 2 * D), D ** -0.5),
        "conv_b_pw1": nrm(ks[5], (N_CONV_LAYERS, 2 * D), 0.02),
        "conv_w_dw": nrm(ks[6], (N_CONV_LAYERS, CONV_WIDTH, 1, D), CONV_WIDTH ** -0.5),
        "conv_b_dw": nrm(ks[7], (N_CONV_LAYERS, D), 0.02),
        "conv_ln_g": 1.0 + nrm(ks[8], (N_CONV_LAYERS, D), 0.02),
        "conv_ln_b": nrm(ks[9], (N_CONV_LAYERS, D), 0.02),
        "conv_w_pw2": nrm(ks[10], (N_CONV_LAYERS, D, D), 0.5 * D ** -0.5),
        "conv_b_pw2": nrm(ks[11], (N_CONV_LAYERS, D), 0.02),
        "attn_w_in": nrm(ks[12], (N_ATTN_LAYERS, D, IN_COLS), D ** -0.5),
        "attn_w_o": nrm(ks[13], (N_ATTN_LAYERS, Q_COLS, D), 0.5 * Q_COLS ** -0.5),
        "rel_bias": nrm(ks[14], (NUM_BUCKETS, N_HEADS), 0.5),
        "router_w": nrm(ks[15], (DEPTH, D, N_EXPERTS), D ** -0.5),
        "router_b": nrm(ks[16], (DEPTH, N_EXPERTS), 0.01),
        "moe_w_gu": nrm(ks[17], (DEPTH, N_EXPERTS, D, 2 * D_FF), D ** -0.5),
        "moe_b_gu": nrm(ks[18], (DEPTH, N_EXPERTS, 2 * D_FF), 0.02),
        "moe_w_down": nrm(ks[19], (DEPTH, N_EXPERTS, D_FF, D), 0.5 * D_FF ** -0.5),
        "moe_b_down": nrm(ks[20], (DEPTH, N_EXPERTS, D), 0.02),
    }


def reference(x, norm_mix, norm_ffn, final_norm,
              conv_w_pw1, conv_b_pw1, conv_w_dw, conv_b_dw, conv_ln_g, conv_ln_b,
              conv_w_pw2, conv_b_pw2,
              attn_w_in, attn_w_o, rel_bias,
              router_w, router_b, moe_w_gu, moe_b_gu, moe_w_down, moe_b_down):
    for i in range(DEPTH):
        hn = rmsnorm(x, norm_mix[i])
        j = i // N_MIXERS
        if i % N_MIXERS == 0:
            y = conformer_conv(hn, conv_w_pw1[j], conv_b_pw1[j], conv_w_dw[j], conv_b_dw[j],
                               conv_ln_g[j], conv_ln_b[j], conv_w_pw2[j], conv_b_pw2[j])
        else:
            y = dsa_attention(hn, attn_w_in[j], attn_w_o[j], rel_bias)
        x = x + y
        x = x + moe_ffn(rmsnorm(x, norm_ffn[i]), router_w[i], router_b[i],
                        moe_w_gu[i], moe_b_gu[i], moe_w_down[i], moe_b_down[i])
    return rmsnorm(x, final_norm)
```

```python
import functools
import math

import numpy as np
import jax
import jax.numpy as jnp
from jax import lax
from jax.experimental import pallas as pl
from jax.experimental.pallas import tpu as pltpu

F32 = jnp.float32
BF16 = jnp.bfloat16
I32 = jnp.int32

NORM_EPS = 1e-5
CONV_WIDTH = 31
HEAD_DIM = 128
N_KV_HEADS = 4
IDX_HEADS = 16
IDX_DIM = 64
INDEX_TOPK_MAX = 256
NUM_BUCKETS = 32
MAX_DISTANCE = 128
N_EXPERTS = 32
TOP_K = 4
SWIGLU_LIMIT = 7.0
SWIGLU_ALPHA = 1.702
MOE_BLOCK = 256
MOE_ITEM_BLOCKS = 4
MOE_TF = 256
ATT_TILE = 256
CONV_HALO = 32
NEG_BIG = -1e30
INT_MIN = -2147483648
VMEM_LIMIT = 56 * 1024 * 1024


def _cparams(sem):
    return pltpu.CompilerParams(dimension_semantics=sem, vmem_limit_bytes=VMEM_LIMIT)


def _rms(x, g):
    ms = jnp.mean(x * x, axis=-1, keepdims=True)
    return (x * lax.rsqrt(ms + NORM_EPS)) * g


def _sigmoid(x):
    return 1.0 / (1.0 + jnp.exp(-x))


def _norm_mm_kernel(x_ref, g_ref, w_ref, b_ref, s_ref, o_ref, hn_ref):
    @pl.when(pl.program_id(1) == 0)
    def _():
        hn_ref[...] = _rms(x_ref[...], g_ref[...]).astype(BF16)

    acc = jnp.dot(hn_ref[...], w_ref[...], preferred_element_type=F32)
    o_ref[...] = ((acc + b_ref[...]) * s_ref[...]).astype(o_ref.dtype)


def norm_matmul(x, g, w, b, s, out_dtype, tm=512, tn=512):
    t, d = x.shape
    n = w.shape[1]
    tn = min(tn, n)
    return pl.pallas_call(
        _norm_mm_kernel,
        grid=(t // tm, n // tn),
        in_specs=[
            pl.BlockSpec((tm, d), lambda i, j: (i, 0)),
            pl.BlockSpec((1, d), lambda i, j: (0, 0)),
            pl.BlockSpec((d, tn), lambda i, j: (0, j)),
            pl.BlockSpec((1, tn), lambda i, j: (0, j)),
            pl.BlockSpec((1, tn), lambda i, j: (0, j)),
        ],
        out_specs=pl.BlockSpec((tm, tn), lambda i, j: (i, j)),
        out_shape=jax.ShapeDtypeStruct((t, n), out_dtype),
        scratch_shapes=[pltpu.VMEM((tm, d), BF16)],
        compiler_params=_cparams(("parallel", "arbitrary")),
        name="norm_matmul",
    )(x, g, w, b, s)


def _norm_mm_glu_kernel(x_ref, g_ref, wa_ref, wg_ref, ba_ref, bg_ref, o_ref, hn_ref):
    @pl.when(pl.program_id(1) == 0)
    def _():
        hn_ref[...] = _rms(x_ref[...], g_ref[...]).astype(BF16)

    hn = hn_ref[...]
    a = jnp.dot(hn, wa_ref[...], preferred_element_type=F32) + ba_ref[...]
    gt = jnp.dot(hn, wg_ref[...], preferred_element_type=F32) + bg_ref[...]
    o_ref[...] = (a * _sigmoid(gt)).astype(o_ref.dtype)


def norm_matmul_glu(x, g, w, b, tm=512, tn=512):
    t, d = x.shape
    n = w.shape[1] // 2
    nj = n // tn
    return pl.pallas_call(
        _norm_mm_glu_kernel,
        grid=(t // tm, nj),
        in_specs=[
            pl.BlockSpec((tm, d), lambda i, j: (i, 0)),
            pl.BlockSpec((1, d), lambda i, j: (0, 0)),
            pl.BlockSpec((d, tn), lambda i, j: (0, j)),
            pl.BlockSpec((d, tn), lambda i, j: (0, j + nj)),
            pl.BlockSpec((1, tn), lambda i, j: (0, j)),
            pl.BlockSpec((1, tn), lambda i, j: (0, j + nj)),
        ],
        out_specs=pl.BlockSpec((tm, tn), lambda i, j: (i, j)),
        out_shape=jax.ShapeDtypeStruct((t, n), BF16),
        scratch_shapes=[pltpu.VMEM((tm, d), BF16)],
        compiler_params=_cparams(("parallel", "arbitrary")),
        name="norm_pw1_glu",
    )(x, g, w, w, b, b)


def _mm_res_kernel(a_ref, w_ref, b_ref, r_ref, o_ref):
    acc = jnp.dot(a_ref[...], w_ref[...], preferred_element_type=F32)
    o_ref[...] = r_ref[...] + (acc + b_ref[...])


def matmul_residual(a, w, b, res, tm=512, tn=512):
    t, k = a.shape
    n = w.shape[1]
    return pl.pallas_call(
        _mm_res_kernel,
        grid=(t // tm, n // tn),
        in_specs=[
            pl.BlockSpec((tm, k), lambda i, j: (i, 0)),
            pl.BlockSpec((k, tn), lambda i, j: (0, j)),
            pl.BlockSpec((1, tn), lambda i, j: (0, j)),
            pl.BlockSpec((tm, tn), lambda i, j: (i, j)),
        ],
        out_specs=pl.BlockSpec((tm, tn), lambda i, j: (i, j)),
        out_shape=jax.ShapeDtypeStruct((t, n), F32),
        compiler_params=_cparams(("parallel", "parallel")),
        name="matmul_residual",
    )(a, w, b, res)


def _dwconv_kernel(prev_ref, cur_ref, w_ref, bdw_ref, lg_ref, lb_ref, o_ref, buf_ref, conv_ref,
                   *, ts, tiles_per_seq, col_chunk, row_chunk):
    i = pl.program_id(0)
    first = (i % tiles_per_seq) == 0
    prev = prev_ref[...].astype(F32)
    buf_ref[0:CONV_HALO, :] = jnp.where(first, 0.0, prev)
    buf_ref[CONV_HALO:, :] = cur_ref[...].astype(F32)
    d = cur_ref.shape[1]
    shift = CONV_HALO - (CONV_WIDTH - 1)

    def col_body(c, carry):
        col = pl.ds(pl.multiple_of(c * col_chunk, col_chunk), col_chunk)
        for r0 in range(0, ts, row_chunk):
            acc = jnp.broadcast_to(bdw_ref[:, col], (row_chunk, col_chunk))
            for k in range(CONV_WIDTH):
                acc = acc + w_ref[k:k + 1, col] * buf_ref[pl.ds(r0 + k + shift, row_chunk), col]
            conv_ref[pl.ds(r0, row_chunk), col] = acc
        return carry

    lax.fori_loop(0, d // col_chunk, col_body, 0)
    y = conv_ref[...]
    mu = jnp.mean(y, axis=-1, keepdims=True)
    yc = y - mu
    var = jnp.mean(yc * yc, axis=-1, keepdims=True)
    z = (yc * lax.rsqrt(var + NORM_EPS)) * lg_ref[...] + lb_ref[...]
    o_ref[...] = (z * _sigmoid(z)).astype(o_ref.dtype)


def dwconv_ln_swish(u, w_dw, b_dw, ln_g, ln_b, seq, ts=256):
    t, d = u.shape
    hb = ts // CONV_HALO
    kern = functools.partial(_dwconv_kernel, ts=ts, tiles_per_seq=seq // ts, col_chunk=512, row_chunk=64)
    return pl.pallas_call(
        kern,
        grid=(t // ts,),
        in_specs=[
            pl.BlockSpec((CONV_HALO, d), lambda i: (jnp.maximum(i * hb - 1, 0), 0)),
            pl.BlockSpec((ts, d), lambda i: (i, 0)),
            pl.BlockSpec((CONV_HALO, d), lambda i: (0, 0)),
            pl.BlockSpec((1, d), lambda i: (0, 0)),
            pl.BlockSpec((1, d), lambda i: (0, 0)),
            pl.BlockSpec((1, d), lambda i: (0, 0)),
        ],
        out_specs=pl.BlockSpec((ts, d), lambda i: (i, 0)),
        out_shape=jax.ShapeDtypeStruct((t, d), BF16),
        scratch_shapes=[pltpu.VMEM((ts + CONV_HALO, d), F32), pltpu.VMEM((ts, d), F32)],
        compiler_params=_cparams(("parallel",)),
        name="dwconv_ln_swish",
    )(u, u, w_dw, b_dw, ln_g, ln_b)


def _router_kernel(x_ref, g_ref, rw_ref, rb_ref, hn_ref, gate_ref, sel_ref, idx_ref):
    hn = _rms(x_ref[...], g_ref[...])
    hn_ref[...] = hn.astype(BF16)
    logits = jnp.dot(hn, rw_ref[...], preferred_element_type=F32,
                     precision=lax.Precision.HIGHEST) + rb_ref[...]
    tm, ne = logits.shape
    lane = lax.broadcasted_iota(I32, (tm, ne), 1).astype(F32)
    lane4 = lax.broadcasted_iota(I32, (tm, TOP_K), 1)
    work = logits
    vals, hits = [], []
    idx_out = jnp.zeros((tm, TOP_K), F32)
    for k in range(TOP_K):
        m = jnp.max(work, axis=-1, keepdims=True)
        idx = jnp.min(jnp.where(work == m, lane, float(ne)), axis=-1, keepdims=True)
        hit = lane == idx
        vals.append(m)
        hits.append(hit)
        idx_out = jnp.where(lane4 == k, idx, idx_out)
        work = jnp.where(hit, -jnp.inf, work)
    es = [jnp.exp(v - vals[0]) for v in vals]
    denom = es[0] + es[1] + es[2] + es[3]
    gates = jnp.zeros((tm, ne), F32)
    sel = jnp.zeros((tm, ne), I32)
    for k in range(TOP_K):
        gates = jnp.where(hits[k], es[k] / denom, gates)
        sel = jnp.where(hits[k], 1, sel)
    gate_ref[...] = gates
    sel_ref[...] = sel
    idx_ref[...] = idx_out.astype(I32)


def router(x, g, rw, rb, tm=512):
    t, d = x.shape
    ne = rw.shape[1]
    return pl.pallas_call(
        _router_kernel,
        grid=(t // tm,),
        in_specs=[
            pl.BlockSpec((tm, d), lambda i: (i, 0)),
            pl.BlockSpec((1, d), lambda i: (0, 0)),
            pl.BlockSpec((d, ne), lambda i: (0, 0)),
            pl.BlockSpec((1, ne), lambda i: (0, 0)),
        ],
        out_specs=[
            pl.BlockSpec((tm, d), lambda i: (i, 0)),
            pl.BlockSpec((tm, ne), lambda i: (i, 0)),
            pl.BlockSpec((tm, ne), lambda i: (i, 0)),
            pl.BlockSpec((tm, TOP_K), lambda i: (i, 0)),
        ],
        out_shape=[
            jax.ShapeDtypeStruct((t, d), BF16),
            jax.ShapeDtypeStruct((t, ne), F32),
            jax.ShapeDtypeStruct((t, ne), I32),
            jax.ShapeDtypeStruct((t, TOP_K), I32),
        ],
        compiler_params=_cparams(("parallel",)),
        name="moe_router",
    )(x, g, rw, rb)


def _moe_kernel(ie_ref, ib_ref, in_ref, x0, x1, x2, x3, wg_ref, wu_ref, wd_ref, bg_ref, bu_ref, bd_ref,
                o_ref, acc_ref):
    w = pl.program_id(0)
    j = pl.program_id(1)
    nj = pl.num_programs(1)
    nblk = in_ref[w]
    xs = (x0, x1, x2, x3)

    @pl.when(nblk > 0)
    def _():
        wg = wg_ref[0].astype(BF16)
        wu = wu_ref[0].astype(BF16)
        wd = wd_ref[0].astype(BF16)
        bg = bg_ref[0]
        bu = bu_ref[0]
        for r in range(MOE_ITEM_BLOCKS):
            @pl.when(r < nblk)
            def _():
                xr = xs[r][...]
                g = jnp.dot(xr, wg, preferred_element_type=F32) + bg
                u = jnp.dot(xr, wu, preferred_element_type=F32) + bu
                g = jnp.minimum(g, SWIGLU_LIMIT)
                u = jnp.clip(u, -SWIGLU_LIMIT, SWIGLU_LIMIT)
                act = (u + 1.0) * (g * _sigmoid(g * SWIGLU_ALPHA))
                part = jnp.dot(act.astype(BF16), wd, preferred_element_type=F32)

                @pl.when(j == 0)
                def _():
                    acc_ref[r] = part

                @pl.when(j > 0)
                def _():
                    acc_ref[r] = acc_ref[r] + part

                @pl.when(j == nj - 1)
                def _():
                    o_ref[0, r * MOE_BLOCK:(r + 1) * MOE_BLOCK, :] = (acc_ref[r] + bd_ref[0]).astype(o_ref.dtype)


def moe_experts(xs, item_expert, item_blk0, item_nblk, w_gu, b_gu, w_down, b_down):
    p_rows, d = xs.shape
    ne, _, dff2 = w_gu.shape
    dff = dff2 // 2
    nf = dff // MOE_TF
    n_items = item_expert.shape[0]

    def x_map(r):
        return lambda w, j, ie, ib, inb: (ib[w] + jnp.minimum(r, jnp.maximum(inb[w] - 1, 0)), 0)

    def jj(j, inb, w):
        return jnp.where(inb[w] > 0, j, nf - 1)

    b_gu3 = b_gu.reshape(ne, 1, dff2)
    b_down3 = b_down.reshape(ne, 1, d)
    grid_spec = pltpu.PrefetchScalarGridSpec(
        num_scalar_prefetch=3,
        grid=(n_items, nf),
        in_specs=[pl.BlockSpec((MOE_BLOCK, d), x_map(r)) for r in range(MOE_ITEM_BLOCKS)] + [
            pl.BlockSpec((1, d, MOE_TF), lambda w, j, ie, ib, inb: (ie[w], 0, jj(j, inb, w))),
            pl.BlockSpec((1, d, MOE_TF), lambda w, j, ie, ib, inb: (ie[w], 0, nf + jj(j, inb, w))),
            pl.BlockSpec((1, MOE_TF, d), lambda w, j, ie, ib, inb: (ie[w], jj(j, inb, w), 0)),
            pl.BlockSpec((1, 1, MOE_TF), lambda w, j, ie, ib, inb: (ie[w], 0, jj(j, inb, w))),
            pl.BlockSpec((1, 1, MOE_TF), lambda w, j, ie, ib, inb: (ie[w], 0, nf + jj(j, inb, w))),
            pl.BlockSpec((1, 1, d), lambda w, j, ie, ib, inb: (ie[w], 0, 0)),
        ],
        out_specs=pl.BlockSpec((1, MOE_ITEM_BLOCKS * MOE_BLOCK, d), lambda w, j, ie, ib, inb: (w, 0, 0)),
        scratch_shapes=[pltpu.VMEM((MOE_ITEM_BLOCKS, MOE_BLOCK, d), F32)],
    )
    return pl.pallas_call(
        _moe_kernel,
        grid_spec=grid_spec,
        out_shape=jax.ShapeDtypeStruct((n_items, MOE_ITEM_BLOCKS * MOE_BLOCK, d), BF16),
        compiler_params=_cparams(("arbitrary", "arbitrary")),
        name="moe_experts",
    )(item_expert, item_blk0, item_nblk, xs, xs, xs, xs, w_gu, w_gu, w_down, b_gu3, b_gu3, b_down3)


def moe_layer(x, norm_g, router_w, router_b, w_gu, b_gu, w_down, b_down):
    t, d = x.shape
    ne = router_w.shape[1]
    hn, gates, sel, top_idx = router(x, norm_g.reshape(1, d), router_w, router_b.reshape(1, ne))

    n_assign = t * TOP_K
    n_blocks = -(-n_assign // MOE_BLOCK) + ne
    counts = jnp.sum(sel, axis=0)
    rank = jnp.cumsum(sel, axis=0) - sel
    nb = (counts + MOE_BLOCK - 1) // MOE_BLOCK
    blk_start = jnp.cumsum(nb) - nb
    dest_full = blk_start[None, :] * MOE_BLOCK + rank
    dest = jnp.take_along_axis(dest_full, top_idx, axis=1)
    gate4 = jnp.take_along_axis(gates, top_idx, axis=1)
    p_rows = n_blocks * MOE_BLOCK
    row_token = jnp.full((p_rows,), t, I32).at[dest.reshape(-1)].set(
        jnp.repeat(jnp.arange(t, dtype=I32), TOP_K))

    max_items = n_blocks // MOE_ITEM_BLOCKS + ne
    n_it = (nb + MOE_ITEM_BLOCKS - 1) // MOE_ITEM_BLOCKS
    it_cum = jnp.cumsum(n_it)
    total_items = it_cum[-1]
    wids = jnp.arange(max_items, dtype=I32)
    e_of = jnp.minimum(jnp.searchsorted(it_cum, wids, side="right"), ne - 1).astype(I32)
    local = wids - (it_cum - n_it)[e_of]
    n_it_e = jnp.maximum(n_it[e_of], 1)
    base = nb[e_of] // n_it_e
    rem = nb[e_of] % n_it_e
    size = base + (local < rem).astype(I32)
    off = local * base + jnp.minimum(local, rem)
    valid = wids < total_items
    last_e = e_of[jnp.maximum(total_items - 1, 0)]
    item_expert = jnp.where(valid, e_of, last_e).astype(I32)
    item_nblk = jnp.where(valid, size, 0).astype(I32)
    item_blk0 = jnp.where(valid, blk_start[e_of] + off, 0).astype(I32)

    hn_pad = jnp.concatenate([hn, jnp.zeros((1, d), hn.dtype)], axis=0)
    xs = hn_pad[row_token]
    ys = moe_experts(xs, item_expert, item_blk0, item_nblk, w_gu, b_gu, w_down, b_down)
    blk_ids = jnp.arange(n_blocks, dtype=I32)
    item_of_blk = jnp.clip(jnp.searchsorted(item_blk0 + jnp.where(valid, 0, n_blocks * 2),
                                            blk_ids, side="right") - 1, 0, max_items - 1).astype(I32)
    slot_of_blk = blk_ids - item_blk0[item_of_blk]
    out_blk = item_of_blk * MOE_ITEM_BLOCKS + slot_of_blk
    out_row = out_blk[dest // MOE_BLOCK] * MOE_BLOCK + dest % MOE_BLOCK
    yk = ys.reshape(max_items * MOE_ITEM_BLOCKS * MOE_BLOCK, d)[out_row].astype(F32)
    return x + jnp.sum(yk * gate4[:, :, None], axis=1)


def _indexer_kernel(qi_ref, kit_ref, wi_ref, o_ref, key_ref, *, tq, topk):
    i = pl.program_id(1)
    nch = i + 1
    o_ref[...] = jnp.full(o_ref.shape, NEG_BIG, o_ref.dtype)
    row = lax.broadcasted_iota(I32, (tq, tq), 0)
    col = lax.broadcasted_iota(I32, (tq, tq), 1)
    wi = wi_ref[...]

    def chunk_slice(c):
        return pl.ds(pl.multiple_of(c * tq, tq), tq)

    def score_chunk(c, carry):
        cs = chunk_slice(c)
        kc = kit_ref[0, :, cs]
        acc = jnp.zeros((tq, tq), F32)
        for h in range(IDX_HEADS):
            s = jnp.dot(qi_ref[:, h * IDX_DIM:(h + 1) * IDX_DIM], kc, preferred_element_type=F32)
            acc = acc + wi[:, h:h + 1] * jnp.maximum(s, 0.0)
        bits = pltpu.bitcast(acc, I32)
        key = bits ^ ((bits >> 31) & 0x7FFFFFFF)
        key = jnp.where((c < i) | (col <= row), key, INT_MIN)
        key_ref[:, cs] = key
        return carry

    lax.fori_loop(0, nch, score_chunk, 0)

    pos = i * tq + lax.broadcasted_iota(I32, (tq, 1), 0)
    kk = jnp.minimum(pos + 1, topk).astype(F32)

    def cond(st):
        bit, _, _, pending = st
        return (bit >= 0) & (pending > 0.5)

    def body(st):
        bit, v, cntv, _ = st
        cand = v | jnp.left_shift(jnp.int32(1), bit)
        thr = jnp.broadcast_to(cand ^ INT_MIN, (tq, 128))

        def cnt_chunk(c, acc):
            kch = key_ref[:, chunk_slice(c)]
            for q in range(tq // 128):
                acc = acc + jnp.where(kch[:, q * 128:(q + 1) * 128] >= thr, 1.0, 0.0)
            return acc

        acc = lax.fori_loop(0, nch, cnt_chunk, jnp.zeros((tq, 128), F32))
        cnt = jnp.sum(acc, axis=-1, keepdims=True)
        ok = cnt >= kk
        v = jnp.where(ok, cand, v)
        cntv = jnp.where(ok, cnt, cntv)
        pending = jnp.max(jnp.where(cntv != kk, 1.0, 0.0))
        return bit - 1, v, cntv, pending

    init = (jnp.int32(31), jnp.zeros((tq, 1), I32), jnp.full((tq, 1), -1.0, F32), jnp.float32(1.0))
    _, v, _, _ = lax.while_loop(cond, body, init)
    thr = jnp.broadcast_to(v ^ INT_MIN, (tq, tq))

    def out_chunk(c, carry):
        cs = chunk_slice(c)
        o_ref[:, cs] = jnp.where(key_ref[:, cs] >= thr, 0.0, NEG_BIG).astype(o_ref.dtype)
        return carry

    lax.fori_loop(0, nch, out_chunk, 0)


def indexer_mask(proj, kit, wi, batch, seq, qi_col_block):
    t = proj.shape[0]
    tq = ATT_TILE
    nq = seq // tq
    qi_cols = IDX_HEADS * IDX_DIM
    topk = min(INDEX_TOPK_MAX, seq // 4)
    kern = functools.partial(_indexer_kernel, tq=tq, topk=topk)
    return pl.pallas_call(
        kern,
        grid=(batch, nq),
        in_specs=[
            pl.BlockSpec((tq, qi_cols), lambda b, i: (b * nq + i, qi_col_block)),
            pl.BlockSpec((1, IDX_DIM, seq), lambda b, i: (b, 0, 0)),
            pl.BlockSpec((tq, IDX_HEADS), lambda b, i: (b * nq + i, 0)),
        ],
        out_specs=pl.BlockSpec((tq, seq), lambda b, i: (b * nq + i, 0)),
        out_shape=jax.ShapeDtypeStruct((t, seq), BF16),
        scratch_shapes=[pltpu.VMEM((tq, seq), I32)],
        compiler_params=_cparams(("parallel", "parallel")),
        name="dsa_indexer",
    )(proj, kit, wi)


def _attn_kernel(q_ref, k_ref, v_ref, mb_ref, bias_ref, o_ref, acc_ref, m_ref, l_ref, *, n_heads, group):
    i = pl.program_id(1)
    j = pl.program_id(2)
    hd = HEAD_DIM
    tq, tk = mb_ref.shape

    @pl.when(j == 0)
    def _():
        acc_ref[...] = jnp.zeros(acc_ref.shape, F32)
        m_ref[...] = jnp.full(m_ref.shape, NEG_BIG, F32)
        l_ref[...] = jnp.zeros(l_ref.shape, F32)

    def heads(near):
        mb = mb_ref[...].astype(F32)
        off = i - j
        for h in range(n_heads):
            n = h // group
            qh = q_ref[:, h * hd:(h + 1) * hd]
            kn = k_ref[:, n * hd:(n + 1) * hd]
            s = lax.dot_general(qh, kn, (((1,), (1,)), ((), ())), preferred_element_type=F32)
            if near:
                s = s + bias_ref[off, h]
            s = s + mb
            m_prev = m_ref[h]
            m_cur = jnp.max(s, axis=1, keepdims=True)
            m_next = jnp.maximum(m_prev, m_cur)
            alpha = jnp.exp(m_prev - m_next)
            p = jnp.exp(s - jnp.concatenate([m_next] * (tk // 128), axis=1))
            l_ref[h] = alpha * l_ref[h] + jnp.sum(p, axis=1, keepdims=True)
            m_ref[h] = m_next
            pv = jnp.dot(p.astype(BF16), v_ref[:, n * hd:(n + 1) * hd], preferred_element_type=F32)
            acc_ref[:, h * hd:(h + 1) * hd] = acc_ref[:, h * hd:(h + 1) * hd] * alpha + pv

    @pl.when((j <= i) & (i - j < 2))
    def _():
        heads(True)

    @pl.when(i - j >= 2)
    def _():
        heads(False)

    @pl.when(j == i)
    def _():
        for h in range(n_heads):
            o_ref[:, h * hd:(h + 1) * hd] = (acc_ref[:, h * hd:(h + 1) * hd] / l_ref[h]).astype(o_ref.dtype)


def sparse_attention(proj, mask_bias, bias_tiles, batch, seq, n_heads):
    t = proj.shape[0]
    tq = tk = ATT_TILE
    nq = seq // tq
    q_cols = n_heads * HEAD_DIM
    kv_cols = N_KV_HEADS * HEAD_DIM
    k_blk = q_cols // kv_cols
    kern = functools.partial(_attn_kernel, n_heads=n_heads, group=n_heads // N_KV_HEADS)
    return pl.pallas_call(
        kern,
        grid=(batch, nq, nq),
        in_specs=[
            pl.BlockSpec((tq, q_cols), lambda b, i, j: (b * nq + i, 0)),
            pl.BlockSpec((tk, kv_cols), lambda b, i, j: (b * nq + jnp.minimum(j, i), k_blk)),
            pl.BlockSpec((tk, kv_cols), lambda b, i, j: (b * nq + jnp.minimum(j, i), k_blk + 1)),
            pl.BlockSpec((tq, tk), lambda b, i, j: (b * nq + i, jnp.minimum(j, i))),
            pl.BlockSpec((2, n_heads, tq, tk), lambda b, i, j: (0, 0, 0, 0)),
        ],
        out_specs=pl.BlockSpec((tq, q_cols), lambda b, i, j: (b * nq + i, 0)),
        out_shape=jax.ShapeDtypeStruct((t, q_cols), BF16),
        scratch_shapes=[
            pltpu.VMEM((tq, q_cols), F32),
            pltpu.VMEM((n_heads, tq, 128), F32),
            pltpu.VMEM((n_heads, tq, 128), F32),
        ],
        compiler_params=_cparams(("parallel", "parallel", "arbitrary")),
        name="dsa_attention",
    )(proj, proj, proj, mask_bias, bias_tiles)


def _t5_bucket(n):
    n = jnp.maximum(n, 0)
    max_exact = NUM_BUCKETS // 2
    nf = jnp.maximum(n, 1).astype(F32)
    large = max_exact + (jnp.log(nf / max_exact) / math.log(MAX_DISTANCE / max_exact)
                         * (NUM_BUCKETS - max_exact)).astype(I32)
    large = jnp.minimum(large, NUM_BUCKETS - 1)
    return jnp.where(n < max_exact, n, large)


def attention_layer(x, norm_g, w_in, w_o, rel_bias, batch, seq):
    t, d = x.shape
    n_heads = w_o.shape[0] // HEAD_DIM
    q_cols = n_heads * HEAD_DIM
    kv_cols = N_KV_HEADS * HEAD_DIM
    qi_cols = IDX_HEADS * IDX_DIM
    main_cols = q_cols + 2 * kv_cols + qi_cols
    assert main_cols % qi_cols == 0 and q_cols % kv_cols == 0
    tail_cols = 128
    w_main = w_in[:, :main_cols].astype(BF16)
    w_tail = jnp.pad(w_in[:, main_cols:], ((0, 0), (0, tail_cols - (IDX_DIM + IDX_HEADS)))).astype(BF16)
    s_main = jnp.concatenate([jnp.full((q_cols,), HEAD_DIM ** -0.5, F32),
                              jnp.ones((main_cols - q_cols,), F32)]).reshape(1, main_cols)
    s_tail = jnp.concatenate([jnp.ones((IDX_DIM,), F32),
                              jnp.full((IDX_HEADS,), (IDX_HEADS ** -0.5) * (IDX_DIM ** -0.5), F32),
                              jnp.ones((tail_cols - IDX_DIM - IDX_HEADS,), F32)]).reshape(1, tail_cols)
    g2 = norm_g.reshape(1, d)
    proj = norm_matmul(x, g2, w_main, jnp.zeros((1, main_cols), F32), s_main, BF16)
    tail = norm_matmul(x, g2, w_tail, jnp.zeros((1, tail_cols), F32), s_tail, F32)
    kit = tail[:, :IDX_DIM].astype(BF16).reshape(batch, seq, IDX_DIM).transpose(0, 2, 1)
    wi = tail[:, IDX_DIM:IDX_DIM + IDX_HEADS]
    mask_bias = indexer_mask(proj, kit, wi, batch, seq, (q_cols + 2 * kv_cols) // qi_cols)

    tile = ATT_TILE
    assert tile >= MAX_DISTANCE
    dist = jnp.arange(2 * tile, dtype=I32)
    tbl = rel_bias[_t5_bucket(dist)] - rel_bias[NUM_BUCKETS - 1][None, :]
    tbl = jnp.where((dist >= MAX_DISTANCE)[:, None], 0.0, tbl)
    r = jnp.arange(tile, dtype=I32)
    dmat = jnp.stack([r[:, None] - r[None, :], tile + r[:, None] - r[None, :]], axis=0)
    bias_tiles = tbl[jnp.clip(dmat, 0, 2 * tile - 1)].transpose(0, 3, 1, 2)

    o = sparse_attention(proj, mask_bias, bias_tiles, batch, seq, n_heads)
    return matmul_residual(o, w_o.astype(BF16), jnp.zeros((1, d), F32), x)


def conv_layer(x, norm_g, w_pw1, b_pw1, w_dw, b_dw, ln_g, ln_b, w_pw2, b_pw2, seq):
    t, d = x.shape
    u = norm_matmul_glu(x, norm_g.reshape(1, d), w_pw1.astype(BF16), b_pw1.reshape(1, 2 * d))
    w_dw2 = jnp.pad(w_dw[:, 0, :], ((0, CONV_HALO - CONV_WIDTH), (0, 0)))
    v = dwconv_ln_swish(u, w_dw2, b_dw.reshape(1, d), ln_g.reshape(1, d), ln_b.reshape(1, d), seq)
    return matmul_residual(v, w_pw2.astype(BF16), b_pw2.reshape(1, d), x)


def _rmsnorm_kernel(x_ref, g_ref, o_ref):
    o_ref[...] = _rms(x_ref[...], g_ref[...])


def rmsnorm_final(x, g, tm=512):
    t, d = x.shape
    return pl.pallas_call(
        _rmsnorm_kernel,
        grid=(t // tm,),
        in_specs=[pl.BlockSpec((tm, d), lambda i: (i, 0)), pl.BlockSpec((1, d), lambda i: (0, 0))],
        out_specs=pl.BlockSpec((tm, d), lambda i: (i, 0)),
        out_shape=jax.ShapeDtypeStruct((t, d), F32),
        compiler_params=_cparams(("parallel",)),
        name="final_rmsnorm",
    )(x, g.reshape(1, d))


def kernel(x, norm_mix, norm_ffn, final_norm, conv_w_pw1, conv_b_pw1, conv_w_dw, conv_b_dw, conv_ln_g,
           conv_ln_b, conv_w_pw2, conv_b_pw2, attn_w_in, attn_w_o, rel_bias, router_w, router_b, moe_w_gu,
           moe_b_gu, moe_w_down, moe_b_down):
    batch, seq, d = x.shape
    depth = norm_mix.shape[0]
    h = x.reshape(batch * seq, d)
    for i in range(depth):
        jdx = i // 2
        if i % 2 == 0:
            h = conv_layer(h, norm_mix[i], conv_w_pw1[jdx], conv_b_pw1[jdx], conv_w_dw[jdx], conv_b_dw[jdx],
                           conv_ln_g[jdx], conv_ln_b[jdx], conv_w_pw2[jdx], conv_b_pw2[jdx], seq)
        else:
            h = attention_layer(h, norm_mix[i], attn_w_in[jdx], attn_w_o[jdx], rel_bias, batch, seq)
        h = moe_layer(h, norm_ffn[i], router_w[i], router_b[i], moe_w_gu[i], moe_b_gu[i],
                      moe_w_down[i], moe_b_down[i])
    return rmsnorm_final(h, final_norm).reshape(batch, seq, d)
```

```python
import functools
import math

import numpy as np
import jax
import jax.numpy as jnp
from jax import lax
from jax.experimental import pallas as pl
from jax.experimental.pallas import tpu as pltpu

F32 = jnp.float32
BF16 = jnp.bfloat16
I32 = jnp.int32

NORM_EPS = 1e-5
CONV_WIDTH = 31
HEAD_DIM = 128
N_KV_HEADS = 4
IDX_HEADS = 16
IDX_DIM = 64
INDEX_TOPK_MAX = 256
NUM_BUCKETS = 32
MAX_DISTANCE = 128
N_EXPERTS = 32
TOP_K = 4
SWIGLU_LIMIT = 7.0
SWIGLU_ALPHA = 1.702
MOE_BLOCK = 256
MOE_ITEM_BLOCKS = 4
MOE_TFA = 512
MOE_TNB = 512
ATT_TILE = 256
CONV_HALO = 32
NEG_BIG = -1e30
INT_MIN = -2147483648
VMEM_LIMIT = 56 * 1024 * 1024


def _cparams(sem):
    return pltpu.CompilerParams(dimension_semantics=sem, vmem_limit_bytes=VMEM_LIMIT)


def _rms(x, g):
    ms = jnp.mean(x * x, axis=-1, keepdims=True)
    return (x * lax.rsqrt(ms + NORM_EPS)) * g


def _sigmoid(x):
    return 1.0 / (1.0 + jnp.exp(-x))


def _norm_mm_kernel(x_ref, g_ref, w_ref, b_ref, s_ref, o_ref, hn_ref):
    @pl.when(pl.program_id(1) == 0)
    def _():
        hn_ref[...] = _rms(x_ref[...], g_ref[...]).astype(BF16)

    acc = jnp.dot(hn_ref[...], w_ref[...], preferred_element_type=F32)
    o_ref[...] = ((acc + b_ref[...]) * s_ref[...]).astype(o_ref.dtype)


def norm_matmul(x, g, w, b, s, out_dtype, tm=512, tn=512):
    t, d = x.shape
    n = w.shape[1]
    tn = min(tn, n)
    return pl.pallas_call(
        _norm_mm_kernel,
        grid=(t // tm, n // tn),
        in_specs=[
            pl.BlockSpec((tm, d), lambda i, j: (i, 0)),
            pl.BlockSpec((1, d), lambda i, j: (0, 0)),
            pl.BlockSpec((d, tn), lambda i, j: (0, j)),
            pl.BlockSpec((1, tn), lambda i, j: (0, j)),
            pl.BlockSpec((1, tn), lambda i, j: (0, j)),
        ],
        out_specs=pl.BlockSpec((tm, tn), lambda i, j: (i, j)),
        out_shape=jax.ShapeDtypeStruct((t, n), out_dtype),
        scratch_shapes=[pltpu.VMEM((tm, d), BF16)],
        compiler_params=_cparams(("parallel", "arbitrary")),
        name="norm_matmul",
    )(x, g, w, b, s)


def _norm_mm_glu_kernel(x_ref, g_ref, wa_ref, wg_ref, ba_ref, bg_ref, o_ref, hn_ref):
    @pl.when(pl.program_id(1) == 0)
    def _():
        hn_ref[...] = _rms(x_ref[...], g_ref[...]).astype(BF16)

    hn = hn_ref[...]
    a = jnp.dot(hn, wa_ref[...], preferred_element_type=F32) + ba_ref[...]
    gt = jnp.dot(hn, wg_ref[...], preferred_element_type=F32) + bg_ref[...]
    o_ref[...] = (a * _sigmoid(gt)).astype(o_ref.dtype)


def norm_matmul_glu(x, g, w, b, tm=512, tn=512):
    t, d = x.shape
    n = w.shape[1] // 2
    nj = n // tn
    return pl.pallas_call(
        _norm_mm_glu_kernel,
        grid=(t // tm, nj),
        in_specs=[
            pl.BlockSpec((tm, d), lambda i, j: (i, 0)),
            pl.BlockSpec((1, d), lambda i, j: (0, 0)),
            pl.BlockSpec((d, tn), lambda i, j: (0, j)),
            pl.BlockSpec((d, tn), lambda i, j: (0, j + nj)),
            pl.BlockSpec((1, tn), lambda i, j: (0, j)),
            pl.BlockSpec((1, tn), lambda i, j: (0, j + nj)),
        ],
        out_specs=pl.BlockSpec((tm, tn), lambda i, j: (i, j)),
        out_shape=jax.ShapeDtypeStruct((t, n), BF16),
        scratch_shapes=[pltpu.VMEM((tm, d), BF16)],
        compiler_params=_cparams(("parallel", "arbitrary")),
        name="norm_pw1_glu",
    )(x, g, w, w, b, b)


def _mm_res_kernel(a_ref, w_ref, b_ref, r_ref, o_ref):
    acc = jnp.dot(a_ref[...], w_ref[...], preferred_element_type=F32)
    o_ref[...] = r_ref[...] + (acc + b_ref[...])


def matmul_residual(a, w, b, res, tm=512, tn=512):
    t, k = a.shape
    n = w.shape[1]
    return pl.pallas_call(
        _mm_res_kernel,
        grid=(t // tm, n // tn),
        in_specs=[
            pl.BlockSpec((tm, k), lambda i, j: (i, 0)),
            pl.BlockSpec((k, tn), lambda i, j: (0, j)),
            pl.BlockSpec((1, tn), lambda i, j: (0, j)),
            pl.BlockSpec((tm, tn), lambda i, j: (i, j)),
        ],
        out_specs=pl.BlockSpec((tm, tn), lambda i, j: (i, j)),
        out_shape=jax.ShapeDtypeStruct((t, n), F32),
        compiler_params=_cparams(("parallel", "parallel")),
        name="matmul_residual",
    )(a, w, b, res)


def _dwconv_kernel(prev_ref, cur_ref, w_ref, bdw_ref, lg_ref, lb_ref, o_ref, buf_ref, conv_ref,
                   *, ts, tiles_per_seq, col_chunk, row_chunk):
    i = pl.program_id(0)
    first = (i % tiles_per_seq) == 0
    prev = prev_ref[...].astype(F32)
    buf_ref[0:CONV_HALO, :] = jnp.where(first, 0.0, prev)
    buf_ref[CONV_HALO:, :] = cur_ref[...].astype(F32)
    d = cur_ref.shape[1]
    shift = CONV_HALO - (CONV_WIDTH - 1)

    def col_body(c, carry):
        col = pl.ds(pl.multiple_of(c * col_chunk, col_chunk), col_chunk)
        for r0 in range(0, ts, row_chunk):
            acc = jnp.broadcast_to(bdw_ref[:, col], (row_chunk, col_chunk))
            for k in range(CONV_WIDTH):
                acc = acc + w_ref[k:k + 1, col] * buf_ref[pl.ds(r0 + k + shift, row_chunk), col]
            conv_ref[pl.ds(r0, row_chunk), col] = acc
        return carry

    lax.fori_loop(0, d // col_chunk, col_body, 0)
    y = conv_ref[...]
    mu = jnp.mean(y, axis=-1, keepdims=True)
    yc = y - mu
    var = jnp.mean(yc * yc, axis=-1, keepdims=True)
    z = (yc * lax.rsqrt(var + NORM_EPS)) * lg_ref[...] + lb_ref[...]
    o_ref[...] = (z * _sigmoid(z)).astype(o_ref.dtype)


def dwconv_ln_swish(u, w_dw, b_dw, ln_g, ln_b, seq, ts=256):
    t, d = u.shape
    hb = ts // CONV_HALO
    kern = functools.partial(_dwconv_kernel, ts=ts, tiles_per_seq=seq // ts, col_chunk=512, row_chunk=64)
    return pl.pallas_call(
        kern,
        grid=(t // ts,),
        in_specs=[
            pl.BlockSpec((CONV_HALO, d), lambda i: (jnp.maximum(i * hb - 1, 0), 0)),
            pl.BlockSpec((ts, d), lambda i: (i, 0)),
            pl.BlockSpec((CONV_HALO, d), lambda i: (0, 0)),
            pl.BlockSpec((1, d), lambda i: (0, 0)),
            pl.BlockSpec((1, d), lambda i: (0, 0)),
            pl.BlockSpec((1, d), lambda i: (0, 0)),
        ],
        out_specs=pl.BlockSpec((ts, d), lambda i: (i, 0)),
        out_shape=jax.ShapeDtypeStruct((t, d), BF16),
        scratch_shapes=[pltpu.VMEM((ts + CONV_HALO, d), F32), pltpu.VMEM((ts, d), F32)],
        compiler_params=_cparams(("parallel",)),
        name="dwconv_ln_swish",
    )(u, u, w_dw, b_dw, ln_g, ln_b)


def _router_kernel(x_ref, g_ref, rw_ref, rb_ref, hn_ref, gate_ref, sel_ref, idx_ref):
    hn = _rms(x_ref[...], g_ref[...])
    hn_ref[...] = hn.astype(BF16)
    logits = jnp.dot(hn, rw_ref[...], preferred_element_type=F32,
                     precision=lax.Precision.HIGHEST) + rb_ref[...]
    tm, ne = logits.shape
    lane = lax.broadcasted_iota(I32, (tm, ne), 1).astype(F32)
    lane4 = lax.broadcasted_iota(I32, (tm, TOP_K), 1)
    work = logits
    vals, hits = [], []
    idx_out = jnp.zeros((tm, TOP_K), F32)
    for k in range(TOP_K):
        m = jnp.max(work, axis=-1, keepdims=True)
        idx = jnp.min(jnp.where(work == m, lane, float(ne)), axis=-1, keepdims=True)
        hit = lane == idx
        vals.append(m)
        hits.append(hit)
        idx_out = jnp.where(lane4 == k, idx, idx_out)
        work = jnp.where(hit, -jnp.inf, work)
    es = [jnp.exp(v - vals[0]) for v in vals]
    denom = es[0] + es[1] + es[2] + es[3]
    gates = jnp.zeros((tm, ne), F32)
    sel = jnp.zeros((tm, ne), I32)
    for k in range(TOP_K):
        gates = jnp.where(hits[k], es[k] / denom, gates)
        sel = jnp.where(hits[k], 1, sel)
    gate_ref[...] = gates
    sel_ref[...] = sel
    idx_ref[...] = idx_out.astype(I32)


def router(x, g, rw, rb, tm=512):
    t, d = x.shape
    ne = rw.shape[1]
    return pl.pallas_call(
        _router_kernel,
        grid=(t // tm,),
        in_specs=[
            pl.BlockSpec((tm, d), lambda i: (i, 0)),
            pl.BlockSpec((1, d), lambda i: (0, 0)),
            pl.BlockSpec((d, ne), lambda i: (0, 0)),
            pl.BlockSpec((1, ne), lambda i: (0, 0)),
        ],
        out_specs=[
            pl.BlockSpec((tm, d), lambda i: (i, 0)),
            pl.BlockSpec((tm, ne), lambda i: (i, 0)),
            pl.BlockSpec((tm, ne), lambda i: (i, 0)),
            pl.BlockSpec((tm, TOP_K), lambda i: (i, 0)),
        ],
        out_shape=[
            jax.ShapeDtypeStruct((t, d), BF16),
            jax.ShapeDtypeStruct((t, ne), F32),
            jax.ShapeDtypeStruct((t, ne), I32),
            jax.ShapeDtypeStruct((t, TOP_K), I32),
        ],
        compiler_params=_cparams(("parallel",)),
        name="moe_router",
    )(x, g, rw, rb)


def _moe_kernel(ie_ref, ib_ref, in_ref, x0, x1, x2, x3, wg_ref, wu_ref, wd_ref, bg_ref, bu_ref, bd_ref,
                o_ref, act_ref, *, nfa):
    w = pl.program_id(0)
    j = pl.program_id(1)
    nblk = in_ref[w]
    xs = (x0, x1, x2, x3)

    def gate_up(nb):
        wg = wg_ref[0, 0].astype(BF16)
        wu = wu_ref[0, 0].astype(BF16)
        bg = bg_ref[0, 0]
        bu = bu_ref[0, 0]
        cols = pl.ds(pl.multiple_of(j * MOE_TFA, MOE_TFA), MOE_TFA)
        for r in range(nb):
            xr = xs[r][...]
            g = jnp.dot(xr, wg, preferred_element_type=F32) + bg
            u = jnp.dot(xr, wu, preferred_element_type=F32) + bu
            g = jnp.minimum(g, SWIGLU_LIMIT)
            u = jnp.clip(u, -SWIGLU_LIMIT, SWIGLU_LIMIT)
            act = (u + 1.0) * (g * _sigmoid(g * SWIGLU_ALPHA))
            act_ref[r, :, cols] = act.astype(BF16)

    def down(nb):
        wd = wd_ref[0, 0].astype(BF16)
        bd = bd_ref[0, 0]
        for r in range(nb):
            y = jnp.dot(act_ref[r], wd, preferred_element_type=F32) + bd
            o_ref[r * MOE_BLOCK:(r + 1) * MOE_BLOCK, :] = y.astype(o_ref.dtype)

    for nb in range(1, MOE_ITEM_BLOCKS + 1):
        @pl.when((nblk == nb) & (j < nfa))
        def _():
            gate_up(nb)

        @pl.when((nblk == nb) & (j >= nfa))
        def _():
            down(nb)


def moe_experts(xs, item_expert, item_blk0, item_nblk, layer, w_gu, b_gu, w_down, b_down):
    p_rows, d = xs.shape
    depth, ne, _, dff2 = w_gu.shape
    dff = dff2 // 2
    nfa = dff // MOE_TFA
    nfb = d // MOE_TNB
    n_items = item_expert.shape[0]
    item_rows = MOE_ITEM_BLOCKS * MOE_BLOCK

    def x_map(r):
        return lambda w, j, ie, ib, inb: (ib[w] + jnp.minimum(r, jnp.maximum(inb[w] - 1, 0)), 0)

    def ja(w, j, inb):
        return jnp.where(inb[w] > 0, jnp.minimum(j, nfa - 1), nfa - 1)

    def jb(w, j, inb):
        return jnp.where(inb[w] > 0, jnp.maximum(j - nfa, 0), nfb - 1)

    b_gu4 = b_gu.reshape(depth, ne, 1, dff2)
    b_down4 = b_down.reshape(depth, ne, 1, d)
    grid_spec = pltpu.PrefetchScalarGridSpec(
        num_scalar_prefetch=3,
        grid=(n_items, nfa + nfb),
        in_specs=[pl.BlockSpec((MOE_BLOCK, d), x_map(r)) for r in range(MOE_ITEM_BLOCKS)] + [
            pl.BlockSpec((1, 1, d, MOE_TFA), lambda w, j, ie, ib, inb: (layer, ie[w], 0, ja(w, j, inb))),
            pl.BlockSpec((1, 1, d, MOE_TFA), lambda w, j, ie, ib, inb: (layer, ie[w], 0, nfa + ja(w, j, inb))),
            pl.BlockSpec((1, 1, dff, MOE_TNB), lambda w, j, ie, ib, inb: (layer, ie[w], 0, jb(w, j, inb))),
            pl.BlockSpec((1, 1, 1, MOE_TFA), lambda w, j, ie, ib, inb: (layer, ie[w], 0, ja(w, j, inb))),
            pl.BlockSpec((1, 1, 1, MOE_TFA), lambda w, j, ie, ib, inb: (layer, ie[w], 0, nfa + ja(w, j, inb))),
            pl.BlockSpec((1, 1, 1, MOE_TNB), lambda w, j, ie, ib, inb: (layer, ie[w], 0, jb(w, j, inb))),
        ],
        out_specs=pl.BlockSpec((item_rows, MOE_TNB), lambda w, j, ie, ib, inb: (w, jb(w, j, inb))),
        scratch_shapes=[pltpu.VMEM((MOE_ITEM_BLOCKS, MOE_BLOCK, dff), BF16)],
    )
    return pl.pallas_call(
        functools.partial(_moe_kernel, nfa=nfa),
        grid_spec=grid_spec,
        out_shape=jax.ShapeDtypeStruct((n_items * item_rows, d), BF16),
        compiler_params=_cparams(("arbitrary", "arbitrary")),
        name="moe_experts",
    )(item_expert, item_blk0, item_nblk, xs, xs, xs, xs, w_gu, w_gu, w_down, b_gu4, b_gu4, b_down4)


def moe_layer(x, norm_g, router_w, router_b, layer, w_gu, b_gu, w_down, b_down):
    t, d = x.shape
    ne = router_w.shape[1]
    hn, gates, sel, top_idx = router(x, norm_g.reshape(1, d), router_w, router_b.reshape(1, ne))

    n_assign = t * TOP_K
    n_blocks = -(-n_assign // MOE_BLOCK) + ne
    counts = jnp.sum(sel, axis=0)
    rank = jnp.cumsum(sel, axis=0) - sel
    nb = (counts + MOE_BLOCK - 1) // MOE_BLOCK
    blk_start = jnp.cumsum(nb) - nb
    dest_full = blk_start[None, :] * MOE_BLOCK + rank
    dest = jnp.take_along_axis(dest_full, top_idx, axis=1)
    gate4 = jnp.take_along_axis(gates, top_idx, axis=1)
    p_rows = n_blocks * MOE_BLOCK
    row_token = jnp.full((p_rows,), t, I32).at[dest.reshape(-1)].set(
        jnp.repeat(jnp.arange(t, dtype=I32), TOP_K))

    max_items = n_blocks // MOE_ITEM_BLOCKS + ne
    n_it = (nb + MOE_ITEM_BLOCKS - 1) // MOE_ITEM_BLOCKS
    it_cum = jnp.cumsum(n_it)
    total_items = it_cum[-1]
    wids = jnp.arange(max_items, dtype=I32)
    e_of = jnp.minimum(jnp.searchsorted(it_cum, wids, side="right"), ne - 1).astype(I32)
    local = wids - (it_cum - n_it)[e_of]
    n_it_e = jnp.maximum(n_it[e_of], 1)
    base = nb[e_of] // n_it_e
    rem = nb[e_of] % n_it_e
    size = base + (local < rem).astype(I32)
    off = local * base + jnp.minimum(local, rem)
    valid = wids < total_items
    last_e = e_of[jnp.maximum(total_items - 1, 0)]
    item_expert = jnp.where(valid, e_of, last_e).astype(I32)
    item_nblk = jnp.where(valid, size, 0).astype(I32)
    item_blk0 = jnp.where(valid, blk_start[e_of] + off, 0).astype(I32)

    hn_pad = jnp.concatenate([hn, jnp.zeros((1, d), hn.dtype)], axis=0)
    xs = hn_pad[row_token]
    ys = moe_experts(xs, item_expert, item_blk0, item_nblk, layer, w_gu, b_gu, w_down, b_down)
    blk_ids = jnp.arange(n_blocks, dtype=I32)
    item_of_blk = jnp.clip(jnp.searchsorted(item_blk0 + jnp.where(valid, 0, n_blocks * 2),
                                            blk_ids, side="right") - 1, 0, max_items - 1).astype(I32)
    slot_of_blk = blk_ids - item_blk0[item_of_blk]
    out_blk = item_of_blk * MOE_ITEM_BLOCKS + slot_of_blk
    out_row = out_blk[dest // MOE_BLOCK] * MOE_BLOCK + dest % MOE_BLOCK
    yk = ys[out_row].astype(F32)
    return x + jnp.sum(yk * gate4[:, :, None], axis=1)


def _indexer_kernel(qi_ref, kit_ref, wi_ref, o_ref, key_ref, wb_ref, *, tq, topk):
    i = pl.program_id(1)
    nch = i + 1
    o_ref[...] = jnp.full(o_ref.shape, NEG_BIG, o_ref.dtype)
    row = lax.broadcasted_iota(I32, (tq, tq), 0)
    col = lax.broadcasted_iota(I32, (tq, tq), 1)
    wi = wi_ref[...]
    for h in range(IDX_HEADS):
        wb_ref[h] = jnp.broadcast_to(wi[:, h:h + 1], (tq, 128))

    def chunk_slice(c):
        return pl.ds(pl.multiple_of(c * tq, tq), tq)

    def score_chunk(c, carry):
        cs = chunk_slice(c)
        kc = kit_ref[0, :, cs]
        acc = jnp.zeros((tq, tq), F32)
        for h in range(IDX_HEADS):
            s = jnp.dot(qi_ref[:, h * IDX_DIM:(h + 1) * IDX_DIM], kc, preferred_element_type=F32)
            wh = wb_ref[h]
            acc = acc + jnp.concatenate([wh] * (tq // 128), axis=1) * jnp.maximum(s, 0.0)
        bits = pltpu.bitcast(acc, I32)
        key = bits ^ ((bits >> 31) & 0x7FFFFFFF)
        key = jnp.where((c < i) | (col <= row), key, INT_MIN)
        key_ref[:, cs] = key
        return carry

    lax.fori_loop(0, nch, score_chunk, 0)

    pos = i * tq + lax.broadcasted_iota(I32, (tq, 1), 0)
    kk = jnp.minimum(pos + 1, topk).astype(F32)

    def cond(st):
        bit, _, _, pending = st
        return (bit >= 0) & (pending > 0.5)

    def body(st):
        bit, v, cntv, _ = st
        cand = v | jnp.left_shift(jnp.int32(1), bit)
        thr = jnp.broadcast_to(cand ^ INT_MIN, (tq, 128))

        def cnt_chunk(c, acc):
            kch = key_ref[:, chunk_slice(c)]
            for q in range(tq // 128):
                acc = acc + jnp.where(kch[:, q * 128:(q + 1) * 128] >= thr, 1.0, 0.0)
            return acc

        acc = lax.fori_loop(0, nch, cnt_chunk, jnp.zeros((tq, 128), F32))
        cnt = jnp.sum(acc, axis=-1, keepdims=True)
        ok = cnt >= kk
        v = jnp.where(ok, cand, v)
        cntv = jnp.where(ok, cnt, cntv)
        pending = jnp.max(jnp.where(cntv != kk, 1.0, 0.0))
        return bit - 1, v, cntv, pending

    init = (jnp.int32(31), jnp.zeros((tq, 1), I32), jnp.full((tq, 1), -1.0, F32), jnp.float32(1.0))
    _, v, _, _ = lax.while_loop(cond, body, init)
    thr = jnp.broadcast_to(v ^ INT_MIN, (tq, tq))

    def out_chunk(c, carry):
        cs = chunk_slice(c)
        o_ref[:, cs] = jnp.where(key_ref[:, cs] >= thr, 0.0, NEG_BIG).astype(o_ref.dtype)
        return carry

    lax.fori_loop(0, nch, out_chunk, 0)


def indexer_mask(proj, kit, wi, batch, seq, qi_col_block):
    t = proj.shape[0]
    tq = ATT_TILE
    nq = seq // tq
    qi_cols = IDX_HEADS * IDX_DIM
    topk = min(INDEX_TOPK_MAX, seq // 4)
    kern = functools.partial(_indexer_kernel, tq=tq, topk=topk)
    return pl.pallas_call(
        kern,
        grid=(batch, nq),
        in_specs=[
            pl.BlockSpec((tq, qi_cols), lambda b, i: (b * nq + i, qi_col_block)),
            pl.BlockSpec((1, IDX_DIM, seq), lambda b, i: (b, 0, 0)),
            pl.BlockSpec((tq, IDX_HEADS), lambda b, i: (b * nq + i, 0)),
        ],
        out_specs=pl.BlockSpec((tq, seq), lambda b, i: (b * nq + i, 0)),
        out_shape=jax.ShapeDtypeStruct((t, seq), BF16),
        scratch_shapes=[pltpu.VMEM((tq, seq), I32), pltpu.VMEM((IDX_HEADS, tq, 128), F32)],
        compiler_params=_cparams(("parallel", "parallel")),
        name="dsa_indexer",
    )(proj, kit, wi)


def _attn_kernel(q_ref, k_ref, v_ref, mb_ref, bias_ref, o_ref, acc_ref, m_ref, l_ref, *, n_heads, group):
    i = pl.program_id(1)
    j = pl.program_id(2)
    hd = HEAD_DIM
    tq, tk = mb_ref.shape

    @pl.when(j == 0)
    def _():
        acc_ref[...] = jnp.zeros(acc_ref.shape, F32)
        m_ref[...] = jnp.full(m_ref.shape, NEG_BIG, F32)
        l_ref[...] = jnp.zeros(l_ref.shape, F32)

    def heads(near):
        mb = mb_ref[...].astype(F32)
        off = i - j
        for h in range(n_heads):
            n = h // group
            qh = q_ref[:, h * hd:(h + 1) * hd]
            kn = k_ref[:, n * hd:(n + 1) * hd]
            s = lax.dot_general(qh, kn, (((1,), (1,)), ((), ())), preferred_element_type=F32)
            if near:
                s = s + bias_ref[off, h]
            s = s + mb
            m_prev = m_ref[h]
            m_cur = jnp.max(s, axis=1, keepdims=True)
            m_next = jnp.maximum(m_prev, m_cur)
            alpha = jnp.exp(m_prev - m_next)
            p = jnp.exp(s - jnp.concatenate([m_next] * (tk // 128), axis=1))
            l_ref[h] = alpha * l_ref[h] + jnp.sum(p, axis=1, keepdims=True)
            m_ref[h] = m_next
            pv = jnp.dot(p.astype(BF16), v_ref[:, n * hd:(n + 1) * hd], preferred_element_type=F32)
            acc_ref[:, h * hd:(h + 1) * hd] = acc_ref[:, h * hd:(h + 1) * hd] * alpha + pv

    @pl.when((j <= i) & (i - j < 2))
    def _():
        heads(True)

    @pl.when(i - j >= 2)
    def _():
        heads(False)

    @pl.when(j == i)
    def _():
        for h in range(n_heads):
            o_ref[:, h * hd:(h + 1) * hd] = (acc_ref[:, h * hd:(h + 1) * hd] / l_ref[h]).astype(o_ref.dtype)


def sparse_attention(proj, mask_bias, bias_tiles, batch, seq, n_heads):
    t = proj.shape[0]
    tq = tk = ATT_TILE
    nq = seq // tq
    q_cols = n_heads * HEAD_DIM
    kv_cols = N_KV_HEADS * HEAD_DIM
    k_blk = q_cols // kv_cols
    kern = functools.partial(_attn_kernel, n_heads=n_heads, group=n_heads // N_KV_HEADS)
    return pl.pallas_call(
        kern,
        grid=(batch, nq, nq),
        in_specs=[
            pl.BlockSpec((tq, q_cols), lambda b, i, j: (b * nq + i, 0)),
            pl.BlockSpec((tk, kv_cols), lambda b, i, j: (b * nq + jnp.minimum(j, i), k_blk)),
            pl.BlockSpec((tk, kv_cols), lambda b, i, j: (b * nq + jnp.minimum(j, i), k_blk + 1)),
            pl.BlockSpec((tq, tk), lambda b, i, j: (b * nq + i, jnp.minimum(j, i))),
            pl.BlockSpec((2, n_heads, tq, tk), lambda b, i, j: (0, 0, 0, 0)),
        ],
        out_specs=pl.BlockSpec((tq, q_cols), lambda b, i, j: (b * nq + i, 0)),
        out_shape=jax.ShapeDtypeStruct((t, q_cols), BF16),
        scratch_shapes=[
            pltpu.VMEM((tq, q_cols), F32),
            pltpu.VMEM((n_heads, tq, 128), F32),
            pltpu.VMEM((n_heads, tq, 128), F32),
        ],
        compiler_params=_cparams(("parallel", "parallel", "arbitrary")),
        name="dsa_attention",
    )(proj, proj, proj, mask_bias, bias_tiles)


def _t5_bucket(n):
    n = jnp.maximum(n, 0)
    max_exact = NUM_BUCKETS // 2
    nf = jnp.maximum(n, 1).astype(F32)
    large = max_exact + (jnp.log(nf / max_exact) / math.log(MAX_DISTANCE / max_exact)
                         * (NUM_BUCKETS - max_exact)).astype(I32)
    large = jnp.minimum(large, NUM_BUCKETS - 1)
    return jnp.where(n < max_exact, n, large)


def attention_layer(x, norm_g, w_in, w_o, rel_bias, batch, seq):
    t, d = x.shape
    n_heads = w_o.shape[0] // HEAD_DIM
    q_cols = n_heads * HEAD_DIM
    kv_cols = N_KV_HEADS * HEAD_DIM
    qi_cols = IDX_HEADS * IDX_DIM
    main_cols = q_cols + 2 * kv_cols + qi_cols
    assert main_cols % qi_cols == 0 and q_cols % kv_cols == 0
    tail_cols = 128
    w_main = w_in[:, :main_cols].astype(BF16)
    w_tail = jnp.pad(w_in[:, main_cols:], ((0, 0), (0, tail_cols - (IDX_DIM + IDX_HEADS)))).astype(BF16)
    s_main = jnp.concatenate([jnp.full((q_cols,), HEAD_DIM ** -0.5, F32),
                              jnp.ones((main_cols - q_cols,), F32)]).reshape(1, main_cols)
    s_tail = jnp.concatenate([jnp.ones((IDX_DIM,), F32),
                              jnp.full((IDX_HEADS,), (IDX_HEADS ** -0.5) * (IDX_DIM ** -0.5), F32),
                              jnp.ones((tail_cols - IDX_DIM - IDX_HEADS,), F32)]).reshape(1, tail_cols)
    g2 = norm_g.reshape(1, d)
    proj = norm_matmul(x, g2, w_main, jnp.zeros((1, main_cols), F32), s_main, BF16)
    tail = norm_matmul(x, g2, w_tail, jnp.zeros((1, tail_cols), F32), s_tail, F32)
    kit = tail[:, :IDX_DIM].astype(BF16).reshape(batch, seq, IDX_DIM).transpose(0, 2, 1)
    wi = tail[:, IDX_DIM:IDX_DIM + IDX_HEADS]
    mask_bias = indexer_mask(proj, kit, wi, batch, seq, (q_cols + 2 * kv_cols) // qi_cols)

    tile = ATT_TILE
    assert tile >= MAX_DISTANCE
    dist = jnp.arange(2 * tile, dtype=I32)
    tbl = rel_bias[_t5_bucket(dist)] - rel_bias[NUM_BUCKETS - 1][None, :]
    tbl = jnp.where((dist >= MAX_DISTANCE)[:, None], 0.0, tbl)
    r = jnp.arange(tile, dtype=I32)
    dmat = jnp.stack([r[:, None] - r[None, :], tile + r[:, None] - r[None, :]], axis=0)
    bias_tiles = tbl[jnp.clip(dmat, 0, 2 * tile - 1)].transpose(0, 3, 1, 2)

    o = sparse_attention(proj, mask_bias, bias_tiles, batch, seq, n_heads)
    return matmul_residual(o, w_o.astype(BF16), jnp.zeros((1, d), F32), x)


def conv_layer(x, norm_g, w_pw1, b_pw1, w_dw, b_dw, ln_g, ln_b, w_pw2, b_pw2, seq):
    t, d = x.shape
    u = norm_matmul_glu(x, norm_g.reshape(1, d), w_pw1.astype(BF16), b_pw1.reshape(1, 2 * d))
    w_dw2 = jnp.pad(w_dw[:, 0, :], ((0, CONV_HALO - CONV_WIDTH), (0, 0)))
    v = dwconv_ln_swish(u, w_dw2, b_dw.reshape(1, d), ln_g.reshape(1, d), ln_b.reshape(1, d), seq)
    return matmul_residual(v, w_pw2.astype(BF16), b_pw2.reshape(1, d), x)


def _rmsnorm_kernel(x_ref, g_ref, o_ref):
    o_ref[...] = _rms(x_ref[...], g_ref[...])


def rmsnorm_final(x, g, tm=512):
    t, d = x.shape
    return pl.pallas_call(
        _rmsnorm_kernel,
        grid=(t // tm,),
        in_specs=[pl.BlockSpec((tm, d), lambda i: (i, 0)), pl.BlockSpec((1, d), lambda i: (0, 0))],
        out_specs=pl.BlockSpec((tm, d), lambda i: (i, 0)),
        out_shape=jax.ShapeDtypeStruct((t, d), F32),
        compiler_params=_cparams(("parallel",)),
        name="final_rmsnorm",
    )(x, g.reshape(1, d))


def kernel(x, norm_mix, norm_ffn, final_norm, conv_w_pw1, conv_b_pw1, conv_w_dw, conv_b_dw, conv_ln_g,
           conv_ln_b, conv_w_pw2, conv_b_pw2, attn_w_in, attn_w_o, rel_bias, router_w, router_b, moe_w_gu,
           moe_b_gu, moe_w_down, moe_b_down):
    batch, seq, d = x.shape
    depth = norm_mix.shape[0]
    h = x.reshape(batch * seq, d)
    for i in range(depth):
        jdx = i // 2
        if i % 2 == 0:
            h = conv_layer(h, norm_mix[i], conv_w_pw1[jdx], conv_b_pw1[jdx], conv_w_dw[jdx], conv_b_dw[jdx],
                           conv_ln_g[jdx], conv_ln_b[jdx], conv_w_pw2[jdx], conv_b_pw2[jdx], seq)
        else:
            h = attention_layer(h, norm_mix[i], attn_w_in[jdx], attn_w_o[jdx], rel_bias, batch, seq)
        h = moe_layer(h, norm_ffn[i], router_w[i], router_b[i], i, moe_w_gu, moe_b_gu, moe_w_down, moe_b_down)
    return rmsnorm_final(h, final_norm).reshape(batch, seq, d)
```

```python
import functools
import math

import numpy as np
import jax
import jax.numpy as jnp
from jax import lax
from jax.experimental import pallas as pl
from jax.experimental.pallas import tpu as pltpu

F32 = jnp.float32
BF16 = jnp.bfloat16
I32 = jnp.int32

NORM_EPS = 1e-5
CONV_WIDTH = 31
HEAD_DIM = 128
N_KV_HEADS = 4
IDX_HEADS = 16
IDX_DIM = 64
INDEX_TOPK_MAX = 256
NUM_BUCKETS = 32
MAX_DISTANCE = 128
N_EXPERTS = 32
TOP_K = 4
SWIGLU_LIMIT = 7.0
SWIGLU_ALPHA = 1.702
MOE_BLOCK = 256
MOE_ITEM_BLOCKS = 10
MOE_TFA = 256
MOE_TNB = 256
ATT_TILE = 256
CONV_HALO = 32
NEG_BIG = -1e30
INT_MIN = -2147483648
VMEM_LIMIT = 56 * 1024 * 1024


def _cparams(sem):
    return pltpu.CompilerParams(dimension_semantics=sem, vmem_limit_bytes=VMEM_LIMIT)


def _rms(x, g):
    ms = jnp.mean(x * x, axis=-1, keepdims=True)
    return (x * lax.rsqrt(ms + NORM_EPS)) * g


def _sigmoid(x):
    return 1.0 / (1.0 + jnp.exp(-x))


def _norm_mm_kernel(x_ref, g_ref, w_ref, b_ref, s_ref, o_ref, hn_ref):
    @pl.when(pl.program_id(1) == 0)
    def _():
        hn_ref[...] = _rms(x_ref[...], g_ref[...]).astype(BF16)

    acc = jnp.dot(hn_ref[...], w_ref[...], preferred_element_type=F32)
    o_ref[...] = ((acc + b_ref[...]) * s_ref[...]).astype(o_ref.dtype)


def norm_matmul(x, g, w, b, s, out_dtype, tm=512, tn=512):
    t, d = x.shape
    n = w.shape[1]
    tn = min(tn, n)
    return pl.pallas_call(
        _norm_mm_kernel,
        grid=(t // tm, n // tn),
        in_specs=[
            pl.BlockSpec((tm, d), lambda i, j: (i, 0)),
            pl.BlockSpec((1, d), lambda i, j: (0, 0)),
            pl.BlockSpec((d, tn), lambda i, j: (0, j)),
            pl.BlockSpec((1, tn), lambda i, j: (0, j)),
            pl.BlockSpec((1, tn), lambda i, j: (0, j)),
        ],
        out_specs=pl.BlockSpec((tm, tn), lambda i, j: (i, j)),
        out_shape=jax.ShapeDtypeStruct((t, n), out_dtype),
        scratch_shapes=[pltpu.VMEM((tm, d), BF16)],
        compiler_params=_cparams(("parallel", "arbitrary")),
        name="norm_matmul",
    )(x, g, w, b, s)


def _norm_mm_glu_kernel(x_ref, g_ref, wa_ref, wg_ref, ba_ref, bg_ref, o_ref, hn_ref):
    @pl.when(pl.program_id(1) == 0)
    def _():
        hn_ref[...] = _rms(x_ref[...], g_ref[...]).astype(BF16)

    hn = hn_ref[...]
    a = jnp.dot(hn, wa_ref[...], preferred_element_type=F32) + ba_ref[...]
    gt = jnp.dot(hn, wg_ref[...], preferred_element_type=F32) + bg_ref[...]
    o_ref[...] = (a * _sigmoid(gt)).astype(o_ref.dtype)


def norm_matmul_glu(x, g, w, b, tm=512, tn=512):
    t, d = x.shape
    n = w.shape[1] // 2
    nj = n // tn
    return pl.pallas_call(
        _norm_mm_glu_kernel,
        grid=(t // tm, nj),
        in_specs=[
            pl.BlockSpec((tm, d), lambda i, j: (i, 0)),
            pl.BlockSpec((1, d), lambda i, j: (0, 0)),
            pl.BlockSpec((d, tn), lambda i, j: (0, j)),
            pl.BlockSpec((d, tn), lambda i, j: (0, j + nj)),
            pl.BlockSpec((1, tn), lambda i, j: (0, j)),
            pl.BlockSpec((1, tn), lambda i, j: (0, j + nj)),
        ],
        out_specs=pl.BlockSpec((tm, tn), lambda i, j: (i, j)),
        out_shape=jax.ShapeDtypeStruct((t, n), BF16),
        scratch_shapes=[pltpu.VMEM((tm, d), BF16)],
        compiler_params=_cparams(("parallel", "arbitrary")),
        name="norm_pw1_glu",
    )(x, g, w, w, b, b)


def _mm_res_kernel(a_ref, w_ref, b_ref, r_ref, o_ref):
    acc = jnp.dot(a_ref[...], w_ref[...], preferred_element_type=F32)
    o_ref[...] = r_ref[...] + (acc + b_ref[...])


def matmul_residual(a, w, b, res, tm=512, tn=512):
    t, k = a.shape
    n = w.shape[1]
    return pl.pallas_call(
        _mm_res_kernel,
        grid=(t // tm, n // tn),
        in_specs=[
            pl.BlockSpec((tm, k), lambda i, j: (i, 0)),
            pl.BlockSpec((k, tn), lambda i, j: (0, j)),
            pl.BlockSpec((1, tn), lambda i, j: (0, j)),
            pl.BlockSpec((tm, tn), lambda i, j: (i, j)),
        ],
        out_specs=pl.BlockSpec((tm, tn), lambda i, j: (i, j)),
        out_shape=jax.ShapeDtypeStruct((t, n), F32),
        compiler_params=_cparams(("parallel", "parallel")),
        name="matmul_residual",
    )(a, w, b, res)


def _dwconv_kernel(prev_ref, cur_ref, w_ref, bdw_ref, lg_ref, lb_ref, o_ref, buf_ref, conv_ref,
                   *, ts, tiles_per_seq, col_chunk, row_chunk):
    i = pl.program_id(0)
    first = (i % tiles_per_seq) == 0
    prev = prev_ref[...].astype(F32)
    buf_ref[0:CONV_HALO, :] = jnp.where(first, 0.0, prev)
    buf_ref[CONV_HALO:, :] = cur_ref[...].astype(F32)
    d = cur_ref.shape[1]
    shift = CONV_HALO - (CONV_WIDTH - 1)

    def col_body(c, carry):
        col = pl.ds(pl.multiple_of(c * col_chunk, col_chunk), col_chunk)
        for r0 in range(0, ts, row_chunk):
            acc = jnp.broadcast_to(bdw_ref[:, col], (row_chunk, col_chunk))
            for k in range(CONV_WIDTH):
                acc = acc + w_ref[k:k + 1, col] * buf_ref[pl.ds(r0 + k + shift, row_chunk), col]
            conv_ref[pl.ds(r0, row_chunk), col] = acc
        return carry

    lax.fori_loop(0, d // col_chunk, col_body, 0)
    y = conv_ref[...]
    mu = jnp.mean(y, axis=-1, keepdims=True)
    yc = y - mu
    var = jnp.mean(yc * yc, axis=-1, keepdims=True)
    z = (yc * lax.rsqrt(var + NORM_EPS)) * lg_ref[...] + lb_ref[...]
    o_ref[...] = (z * _sigmoid(z)).astype(o_ref.dtype)


def dwconv_ln_swish(u, w_dw, b_dw, ln_g, ln_b, seq, ts=256):
    t, d = u.shape
    hb = ts // CONV_HALO
    kern = functools.partial(_dwconv_kernel, ts=ts, tiles_per_seq=seq // ts, col_chunk=512, row_chunk=64)
    return pl.pallas_call(
        kern,
        grid=(t // ts,),
        in_specs=[
            pl.BlockSpec((CONV_HALO, d), lambda i: (jnp.maximum(i * hb - 1, 0), 0)),
            pl.BlockSpec((ts, d), lambda i: (i, 0)),
            pl.BlockSpec((CONV_HALO, d), lambda i: (0, 0)),
            pl.BlockSpec((1, d), lambda i: (0, 0)),
            pl.BlockSpec((1, d), lambda i: (0, 0)),
            pl.BlockSpec((1, d), lambda i: (0, 0)),
        ],
        out_specs=pl.BlockSpec((ts, d), lambda i: (i, 0)),
        out_shape=jax.ShapeDtypeStruct((t, d), BF16),
        scratch_shapes=[pltpu.VMEM((ts + CONV_HALO, d), F32), pltpu.VMEM((ts, d), F32)],
        compiler_params=_cparams(("parallel",)),
        name="dwconv_ln_swish",
    )(u, u, w_dw, b_dw, ln_g, ln_b)


def _router_kernel(x_ref, g_ref, rw_ref, rb_ref, hn_ref, idx_ref, gate_ref, rank_ref, cnt_ref):
    @pl.when(pl.program_id(0) == 0)
    def _():
        cnt_ref[...] = jnp.zeros(cnt_ref.shape, F32)

    hn = _rms(x_ref[...], g_ref[...])
    hn_ref[...] = hn.astype(BF16)
    logits = jnp.dot(hn, rw_ref[...], preferred_element_type=F32,
                     precision=lax.Precision.HIGHEST) + rb_ref[...]
    tm, ne = logits.shape
    lane = lax.broadcasted_iota(I32, (tm, ne), 1).astype(F32)
    lane4 = lax.broadcasted_iota(I32, (tm, TOP_K), 1)
    work = logits
    vals, hits = [], []
    idx_out = jnp.zeros((tm, TOP_K), F32)
    for k in range(TOP_K):
        m = jnp.max(work, axis=-1, keepdims=True)
        idx = jnp.min(jnp.where(work == m, lane, float(ne)), axis=-1, keepdims=True)
        hit = lane == idx
        vals.append(m)
        hits.append(hit)
        idx_out = jnp.where(lane4 == k, idx, idx_out)
        work = jnp.where(hit, -jnp.inf, work)
    es = [jnp.exp(v - vals[0]) for v in vals]
    denom = es[0] + es[1] + es[2] + es[3]
    sel = jnp.zeros((tm, ne), F32)
    gate_out = jnp.zeros((tm, TOP_K), F32)
    for k in range(TOP_K):
        sel = jnp.where(hits[k], 1.0, sel)
        gate_out = jnp.where(lane4 == k, es[k] / denom, gate_out)
    r_i = lax.broadcasted_iota(I32, (tm, tm), 0)
    c_i = lax.broadcasted_iota(I32, (tm, tm), 1)
    tri = jnp.where(c_i < r_i, 1.0, 0.0).astype(BF16)
    rank_full = jnp.dot(tri, sel.astype(BF16), preferred_element_type=F32) + cnt_ref[...]
    rank_out = jnp.zeros((tm, TOP_K), F32)
    for k in range(TOP_K):
        rk = jnp.sum(jnp.where(hits[k], rank_full, 0.0), axis=-1, keepdims=True)
        rank_out = jnp.where(lane4 == k, rk, rank_out)
    cnt_ref[...] = cnt_ref[...] + jnp.sum(sel, axis=0, keepdims=True)
    idx_ref[...] = idx_out.astype(I32)
    gate_ref[...] = gate_out
    rank_ref[...] = rank_out.astype(I32)


def router(x, g, rw, rb, tm=512):
    t, d = x.shape
    ne = rw.shape[1]
    return pl.pallas_call(
        _router_kernel,
        grid=(t // tm,),
        in_specs=[
            pl.BlockSpec((tm, d), lambda i: (i, 0)),
            pl.BlockSpec((1, d), lambda i: (0, 0)),
            pl.BlockSpec((d, ne), lambda i: (0, 0)),
            pl.BlockSpec((1, ne), lambda i: (0, 0)),
        ],
        out_specs=[
            pl.BlockSpec((tm, d), lambda i: (i, 0)),
            pl.BlockSpec((tm, TOP_K), lambda i: (i, 0)),
            pl.BlockSpec((tm, TOP_K), lambda i: (i, 0)),
            pl.BlockSpec((tm, TOP_K), lambda i: (i, 0)),
            pl.BlockSpec((1, ne), lambda i: (0, 0)),
        ],
        out_shape=[
            jax.ShapeDtypeStruct((t, d), BF16),
            jax.ShapeDtypeStruct((t, TOP_K), I32),
            jax.ShapeDtypeStruct((t, TOP_K), F32),
            jax.ShapeDtypeStruct((t, TOP_K), I32),
            jax.ShapeDtypeStruct((1, ne), F32),
        ],
        compiler_params=_cparams(("arbitrary",)),
        name="moe_router",
    )(x, g, rw, rb)


def _moe_kernel(ie_ref, ib_ref, in_ref, x_ref, wg_ref, wu_ref, wd_ref, bg_ref, bu_ref, bd_ref,
                o_ref, act_ref, *, nfa):
    w = pl.program_id(0)
    j = pl.program_id(1)
    nblk = in_ref[w]
    group = 4
    group_rows = group * MOE_BLOCK
    ngroup = nblk // group

    def for_row_groups(fn):
        def body(p, carry):
            fn(pl.multiple_of(p * group_rows, group_rows), group_rows)
            return carry

        lax.fori_loop(0, ngroup, body, 0)
        for rem in range(1, group):
            @pl.when(nblk % group == rem)
            def _():
                fn(pl.multiple_of(ngroup * group_rows, group_rows), rem * MOE_BLOCK)

    @pl.when((nblk > 0) & (j < nfa))
    def _():
        wg = wg_ref[0, 0].astype(BF16)
        wu = wu_ref[0, 0].astype(BF16)
        bg = bg_ref[0, 0]
        bu = bu_ref[0, 0]
        cols = pl.ds(pl.multiple_of(j * MOE_TFA, MOE_TFA), MOE_TFA)

        def gate_up(r0, nrows):
            xr = x_ref[pl.ds(r0, nrows), :]
            g = jnp.dot(xr, wg, preferred_element_type=F32) + bg
            u = jnp.dot(xr, wu, preferred_element_type=F32) + bu
            g = jnp.minimum(g, SWIGLU_LIMIT)
            u = jnp.clip(u, -SWIGLU_LIMIT, SWIGLU_LIMIT)
            act = (u + 1.0) * (g * _sigmoid(g * SWIGLU_ALPHA))
            act_ref[pl.ds(r0, nrows), cols] = act.astype(BF16)

        for_row_groups(gate_up)

    @pl.when((nblk > 0) & (j >= nfa))
    def _():
        wd = wd_ref[0, 0].astype(BF16)
        bd = bd_ref[0, 0]

        def down(r0, nrows):
            y = jnp.dot(act_ref[pl.ds(r0, nrows), :], wd, preferred_element_type=F32) + bd
            o_ref[pl.ds(r0, nrows), :] = y.astype(o_ref.dtype)

        for_row_groups(down)


def moe_experts(xs, item_expert, item_blk0, item_nblk, layer, w_gu, b_gu, w_down, b_down):
    p_rows, d = xs.shape
    depth, ne, _, dff2 = w_gu.shape
    dff = dff2 // 2
    nfa = dff // MOE_TFA
    nfb = d // MOE_TNB
    n_items = item_expert.shape[0]
    item_rows = MOE_ITEM_BLOCKS * MOE_BLOCK

    def ja(w, j, inb):
        return jnp.where(inb[w] > 0, jnp.minimum(j, nfa - 1), nfa - 1)

    def jb(w, j, inb):
        return jnp.where(inb[w] > 0, jnp.maximum(j - nfa, 0), nfb - 1)

    b_gu4 = b_gu.reshape(depth, ne, 1, dff2)
    b_down4 = b_down.reshape(depth, ne, 1, d)
    grid_spec = pltpu.PrefetchScalarGridSpec(
        num_scalar_prefetch=3,
        grid=(n_items, nfa + nfb),
        in_specs=[
            pl.BlockSpec((pl.Element(item_rows), pl.Element(d)),
                         lambda w, j, ie, ib, inb: (ib[w] * MOE_BLOCK, 0)),
            pl.BlockSpec((1, 1, d, MOE_TFA), lambda w, j, ie, ib, inb: (layer, ie[w], 0, ja(w, j, inb))),
            pl.BlockSpec((1, 1, d, MOE_TFA), lambda w, j, ie, ib, inb: (layer, ie[w], 0, nfa + ja(w, j, inb))),
            pl.BlockSpec((1, 1, dff, MOE_TNB), lambda w, j, ie, ib, inb: (layer, ie[w], 0, jb(w, j, inb))),
            pl.BlockSpec((1, 1, 1, MOE_TFA), lambda w, j, ie, ib, inb: (layer, ie[w], 0, ja(w, j, inb))),
            pl.BlockSpec((1, 1, 1, MOE_TFA), lambda w, j, ie, ib, inb: (layer, ie[w], 0, nfa + ja(w, j, inb))),
            pl.BlockSpec((1, 1, 1, MOE_TNB), lambda w, j, ie, ib, inb: (layer, ie[w], 0, jb(w, j, inb))),
        ],
        out_specs=pl.BlockSpec((item_rows, MOE_TNB), lambda w, j, ie, ib, inb: (w, jb(w, j, inb))),
        scratch_shapes=[pltpu.VMEM((item_rows, dff), BF16)],
    )
    return pl.pallas_call(
        functools.partial(_moe_kernel, nfa=nfa),
        grid_spec=grid_spec,
        out_shape=jax.ShapeDtypeStruct((n_items * item_rows, d), BF16),
        compiler_params=_cparams(("arbitrary", "arbitrary")),
        name="moe_experts",
    )(item_expert, item_blk0, item_nblk, xs, w_gu, w_gu, w_down, b_gu4, b_gu4, b_down4)


def _combine_kernel(y_ref, g_ref, x_ref, *rest, final):
    o_ref = rest[-1]
    d = x_ref.shape[1]
    g = g_ref[...]
    acc = x_ref[...]
    for k in range(TOP_K):
        acc = acc + g[:, k:k + 1] * y_ref[:, k * d:(k + 1) * d].astype(F32)
    if final:
        acc = _rms(acc, rest[0][...])
    o_ref[...] = acc


def moe_combine(yk, gate4, x, final_g=None, tm=256):
    t, d = x.shape
    final = final_g is not None
    in_specs = [
        pl.BlockSpec((tm, TOP_K * d), lambda i: (i, 0)),
        pl.BlockSpec((tm, TOP_K), lambda i: (i, 0)),
        pl.BlockSpec((tm, d), lambda i: (i, 0)),
    ]
    args = [yk, gate4, x]
    if final:
        in_specs.append(pl.BlockSpec((1, d), lambda i: (0, 0)))
        args.append(final_g.reshape(1, d))
    return pl.pallas_call(
        functools.partial(_combine_kernel, final=final),
        grid=(t // tm,),
        in_specs=in_specs,
        out_specs=pl.BlockSpec((tm, d), lambda i: (i, 0)),
        out_shape=jax.ShapeDtypeStruct((t, d), F32),
        compiler_params=_cparams(("parallel",)),
        name="moe_combine",
    )(*args)


def _lookup(table, idx):
    onehot = idx[..., None] == jnp.arange(table.shape[0], dtype=I32)
    return jnp.sum(jnp.where(onehot, table, 0), axis=-1)


def moe_layer(x, norm_g, router_w, router_b, layer, w_gu, b_gu, w_down, b_down, final_g=None):
    t, d = x.shape
    ne = router_w.shape[1]
    hn, top_idx, gate4, rank4, cnt = router(x, norm_g.reshape(1, d), router_w, router_b.reshape(1, ne))

    n_assign = t * TOP_K
    n_blocks = -(-n_assign // MOE_BLOCK) + ne
    p_rows = n_blocks * MOE_BLOCK
    item_rows = MOE_ITEM_BLOCKS * MOE_BLOCK
    counts = cnt[0].astype(I32)
    nb = (counts + MOE_BLOCK - 1) // MOE_BLOCK
    blk_start = jnp.cumsum(nb) - nb

    max_items = n_blocks // MOE_ITEM_BLOCKS + ne
    n_it = (nb + MOE_ITEM_BLOCKS - 1) // MOE_ITEM_BLOCKS
    it_cum = jnp.cumsum(n_it)
    it_start = it_cum - n_it
    total_items = it_cum[-1]
    base_e = nb // jnp.maximum(n_it, 1)
    rem_e = nb % jnp.maximum(n_it, 1)
    wids = jnp.arange(max_items, dtype=I32)
    e_of = jnp.minimum(jnp.searchsorted(it_cum, wids, side="right"), ne - 1).astype(I32)
    local = wids - it_start[e_of]
    size = base_e[e_of] + (local < rem_e[e_of]).astype(I32)
    off = local * base_e[e_of] + jnp.minimum(local, rem_e[e_of])
    valid = wids < total_items
    last_e = e_of[jnp.maximum(total_items - 1, 0)]
    item_expert = jnp.where(valid, e_of, last_e).astype(I32)
    item_nblk = jnp.where(valid, size, 0).astype(I32)
    item_blk0 = jnp.where(valid, blk_start[e_of] + off, 0).astype(I32)

    blk_in_e = rank4 // MOE_BLOCK
    within = rank4 % MOE_BLOCK
    a_base = _lookup(base_e, top_idx)
    a_rem = _lookup(rem_e, top_idx)
    dest = (_lookup(blk_start, top_idx) + blk_in_e) * MOE_BLOCK + within
    big = a_base + 1
    n_big = a_rem * big
    in_big = blk_in_e < n_big
    small = jnp.maximum(a_base, 1)
    a_local = jnp.where(in_big, blk_in_e // big, a_rem + (blk_in_e - n_big) // small)
    a_slot = jnp.where(in_big, blk_in_e % big, (blk_in_e - n_big) % small)
    out_row = ((_lookup(it_start, top_idx) + a_local) * MOE_ITEM_BLOCKS + a_slot) * MOE_BLOCK + within

    row_token = jnp.full((p_rows + item_rows,), t, I32).at[dest.reshape(-1)].set(
        jnp.repeat(jnp.arange(t, dtype=I32), TOP_K))
    hn_pad = jnp.concatenate([hn, jnp.zeros((1, d), hn.dtype)], axis=0)
    xs = hn_pad[row_token]
    ys = moe_experts(xs, item_expert, item_blk0, item_nblk, layer, w_gu, b_gu, w_down, b_down)
    yk = ys[out_row.reshape(-1)].reshape(t, TOP_K * d)
    return moe_combine(yk, gate4, x, final_g)


def _indexer_kernel(qi_ref, kit_ref, wi_ref, o_ref, key_ref, wb_ref, *, tq, topk):
    i = pl.program_id(1)
    nch = i + 1
    o_ref[...] = jnp.full(o_ref.shape, NEG_BIG, o_ref.dtype)
    row = lax.broadcasted_iota(I32, (tq, tq), 0)
    col = lax.broadcasted_iota(I32, (tq, tq), 1)
    wi = wi_ref[...]
    for h in range(IDX_HEADS):
        wb_ref[h] = jnp.broadcast_to(wi[:, h:h + 1], (tq, 128))

    def chunk_slice(c):
        return pl.ds(pl.multiple_of(c * tq, tq), tq)

    def score_chunk(c, carry):
        cs = chunk_slice(c)
        kc = kit_ref[0, :, cs]
        acc = jnp.zeros((tq, tq), F32)
        for h in range(IDX_HEADS):
            s = jnp.dot(qi_ref[:, h * IDX_DIM:(h + 1) * IDX_DIM], kc, preferred_element_type=F32)
            wh = wb_ref[h]
            acc = acc + jnp.concatenate([wh] * (tq // 128), axis=1) * jnp.maximum(s, 0.0)
        bits = pltpu.bitcast(acc, I32)
        key = bits ^ ((bits >> 31) & 0x7FFFFFFF)
        key = jnp.where((c < i) | (col <= row), key, INT_MIN)
        key_ref[:, cs] = key
        return carry

    lax.fori_loop(0, nch, score_chunk, 0)

    pos = i * tq + lax.broadcasted_iota(I32, (tq, 1), 0)
    kk = jnp.minimum(pos + 1, topk).astype(F32)

    def cond(st):
        bit, _, _, pending = st
        return (bit >= 0) & (pending > 0.5)

    def body(st):
        bit, v, cntv, _ = st
        cand = v | jnp.left_shift(jnp.int32(1), bit)
        thr = jnp.broadcast_to(cand ^ INT_MIN, (tq, 128))

        def cnt_chunk(c, acc):
            kch = key_ref[:, chunk_slice(c)]
            for q in range(tq // 128):
                acc = acc + jnp.where(kch[:, q * 128:(q + 1) * 128] >= thr, 1.0, 0.0)
            return acc

        acc = lax.fori_loop(0, nch, cnt_chunk, jnp.zeros((tq, 128), F32))
        cnt = jnp.sum(acc, axis=-1, keepdims=True)
        ok = cnt >= kk
        v = jnp.where(ok, cand, v)
        cntv = jnp.where(ok, cnt, cntv)
        pending = jnp.max(jnp.where(cntv != kk, 1.0, 0.0))
        return bit - 1, v, cntv, pending

    init = (jnp.int32(31), jnp.zeros((tq, 1), I32), jnp.full((tq, 1), -1.0, F32), jnp.float32(1.0))
    _, v, _, _ = lax.while_loop(cond, body, init)
    thr = jnp.broadcast_to(v ^ INT_MIN, (tq, tq))

    def out_chunk(c, carry):
        cs = chunk_slice(c)
        o_ref[:, cs] = jnp.where(key_ref[:, cs] >= thr, 0.0, NEG_BIG).astype(o_ref.dtype)
        return carry

    lax.fori_loop(0, nch, out_chunk, 0)


def indexer_mask(proj, kit, wi, batch, seq, qi_col_block):
    t = proj.shape[0]
    tq = ATT_TILE
    nq = seq // tq
    qi_cols = IDX_HEADS * IDX_DIM
    topk = min(INDEX_TOPK_MAX, seq // 4)
    kern = functools.partial(_indexer_kernel, tq=tq, topk=topk)
    return pl.pallas_call(
        kern,
        grid=(batch, nq),
        in_specs=[
            pl.BlockSpec((tq, qi_cols), lambda b, i: (b * nq + i, qi_col_block)),
            pl.BlockSpec((1, IDX_DIM, seq), lambda b, i: (b, 0, 0)),
            pl.BlockSpec((tq, IDX_HEADS), lambda b, i: (b * nq + i, 0)),
        ],
        out_specs=pl.BlockSpec((tq, seq), lambda b, i: (b * nq + i, 0)),
        out_shape=jax.ShapeDtypeStruct((t, seq), BF16),
        scratch_shapes=[pltpu.VMEM((tq, seq), I32), pltpu.VMEM((IDX_HEADS, tq, 128), F32)],
        compiler_params=_cparams(("parallel", "parallel")),
        name="dsa_indexer",
    )(proj, kit, wi)


def _attn_kernel(q_ref, k_ref, v_ref, mb_ref, bias_ref, o_ref, acc_ref, m_ref, l_ref, *, n_heads, group):
    i = pl.program_id(1)
    j = pl.program_id(2)
    hd = HEAD_DIM
    tq, tk = mb_ref.shape

    @pl.when(j == 0)
    def _():
        acc_ref[...] = jnp.zeros(acc_ref.shape, F32)
        m_ref[...] = jnp.full(m_ref.shape, NEG_BIG, F32)
        l_ref[...] = jnp.zeros(l_ref.shape, F32)

    def heads(near):
        mb = mb_ref[...].astype(F32)
        off = i - j
        for h in range(n_heads):
            n = h // group
            qh = q_ref[:, h * hd:(h + 1) * hd]
            kn = k_ref[:, n * hd:(n + 1) * hd]
            s = lax.dot_general(qh, kn, (((1,), (1,)), ((), ())), preferred_element_type=F32)
            if near:
                s = s + bias_ref[off, h]
            s = s + mb
            m_prev = m_ref[h]
            m_cur = jnp.max(s, axis=1, keepdims=True)
            m_next = jnp.maximum(m_prev, m_cur)
            alpha = jnp.exp(m_prev - m_next)
            p = jnp.exp(s - jnp.concatenate([m_next] * (tk // 128), axis=1))
            l_ref[h] = alpha * l_ref[h] + jnp.sum(p, axis=1, keepdims=True)
            m_ref[h] = m_next
            pv = jnp.dot(p.astype(BF16), v_ref[:, n * hd:(n + 1) * hd], preferred_element_type=F32)
            acc_ref[:, h * hd:(h + 1) * hd] = acc_ref[:, h * hd:(h + 1) * hd] * alpha + pv

    @pl.when((j <= i) & (i - j < 2))
    def _():
        heads(True)

    @pl.when(i - j >= 2)
    def _():
        heads(False)

    @pl.when(j == i)
    def _():
        for h in range(n_heads):
            o_ref[:, h * hd:(h + 1) * hd] = (acc_ref[:, h * hd:(h + 1) * hd] / l_ref[h]).astype(o_ref.dtype)


def sparse_attention(proj, mask_bias, bias_tiles, batch, seq, n_heads):
    t = proj.shape[0]
    tq = tk = ATT_TILE
    nq = seq // tq
    q_cols = n_heads * HEAD_DIM
    kv_cols = N_KV_HEADS * HEAD_DIM
    k_blk = q_cols // kv_cols
    kern = functools.partial(_attn_kernel, n_heads=n_heads, group=n_heads // N_KV_HEADS)
    return pl.pallas_call(
        kern,
        grid=(batch, nq, nq),
        in_specs=[
            pl.BlockSpec((tq, q_cols), lambda b, i, j: (b * nq + i, 0)),
            pl.BlockSpec((tk, kv_cols), lambda b, i, j: (b * nq + jnp.minimum(j, i), k_blk)),
            pl.BlockSpec((tk, kv_cols), lambda b, i, j: (b * nq + jnp.minimum(j, i), k_blk + 1)),
            pl.BlockSpec((tq, tk), lambda b, i, j: (b * nq + i, jnp.minimum(j, i))),
            pl.BlockSpec((2, n_heads, tq, tk), lambda b, i, j: (0, 0, 0, 0)),
        ],
        out_specs=pl.BlockSpec((tq, q_cols), lambda b, i, j: (b * nq + i, 0)),
        out_shape=jax.ShapeDtypeStruct((t, q_cols), BF16),
        scratch_shapes=[
            pltpu.VMEM((tq, q_cols), F32),
            pltpu.VMEM((n_heads, tq, 128), F32),
            pltpu.VMEM((n_heads, tq, 128), F32),
        ],
        compiler_params=_cparams(("parallel", "parallel", "arbitrary")),
        name="dsa_attention",
    )(proj, proj, proj, mask_bias, bias_tiles)


def _t5_bucket(n):
    n = jnp.maximum(n, 0)
    max_exact = NUM_BUCKETS // 2
    nf = jnp.maximum(n, 1).astype(F32)
    large = max_exact + (jnp.log(nf / max_exact) / math.log(MAX_DISTANCE / max_exact)
                         * (NUM_BUCKETS - max_exact)).astype(I32)
    large = jnp.minimum(large, NUM_BUCKETS - 1)
    return jnp.where(n < max_exact, n, large)


def _toeplitz(by_delta, n):
    h = by_delta.shape[0]
    u = jnp.concatenate([by_delta[:, :n][:, ::-1], jnp.zeros((h, 2), by_delta.dtype),
                         by_delta[:, n:][:, ::-1]], axis=1)
    rows = jnp.tile(u, (1, n))[:, :n * 2 * n].reshape(h, n, 2 * n)
    return rows[:, :, :n]


def attention_layer(x, norm_g, w_in, w_o, rel_bias, batch, seq):
    t, d = x.shape
    n_heads = w_o.shape[0] // HEAD_DIM
    q_cols = n_heads * HEAD_DIM
    kv_cols = N_KV_HEADS * HEAD_DIM
    qi_cols = IDX_HEADS * IDX_DIM
    main_cols = q_cols + 2 * kv_cols + qi_cols
    assert main_cols % qi_cols == 0 and q_cols % kv_cols == 0
    tail_cols = 128
    w_main = w_in[:, :main_cols].astype(BF16)
    w_tail = jnp.pad(w_in[:, main_cols:], ((0, 0), (0, tail_cols - (IDX_DIM + IDX_HEADS)))).astype(BF16)
    s_main = jnp.concatenate([jnp.full((q_cols,), HEAD_DIM ** -0.5, F32),
                              jnp.ones((main_cols - q_cols,), F32)]).reshape(1, main_cols)
    s_tail = jnp.concatenate([jnp.ones((IDX_DIM,), F32),
                              jnp.full((IDX_HEADS,), (IDX_HEADS ** -0.5) * (IDX_DIM ** -0.5), F32),
                              jnp.ones((tail_cols - IDX_DIM - IDX_HEADS,), F32)]).reshape(1, tail_cols)
    g2 = norm_g.reshape(1, d)
    proj = norm_matmul(x, g2, w_main, jnp.zeros((1, main_cols), F32), s_main, BF16)
    tail = norm_matmul(x, g2, w_tail, jnp.zeros((1, tail_cols), F32), s_tail, F32)
    kit = tail[:, :IDX_DIM].astype(BF16).reshape(batch, seq, IDX_DIM).transpose(0, 2, 1)
    wi = tail[:, IDX_DIM:IDX_DIM + IDX_HEADS]
    mask_bias = indexer_mask(proj, kit, wi, batch, seq, (q_cols + 2 * kv_cols) // qi_cols)

    tile = ATT_TILE
    assert tile >= MAX_DISTANCE
    dist = jnp.arange(2 * tile, dtype=I32)
    onehot = (_t5_bucket(dist)[:, None] == jnp.arange(NUM_BUCKETS, dtype=I32)).astype(F32)
    tbl = jnp.dot(onehot, rel_bias - rel_bias[NUM_BUCKETS - 1][None, :], precision=lax.Precision.HIGHEST)
    tbl = jnp.where((dist >= MAX_DISTANCE)[:, None], 0.0, tbl).T
    by_delta0 = jnp.concatenate([jnp.zeros((n_heads, tile - 1), F32), tbl[:, :tile]], axis=1)
    by_delta1 = tbl[:, 1:]
    bias_tiles = jnp.stack([_toeplitz(by_delta0, tile), _toeplitz(by_delta1, tile)], axis=0)

    o = sparse_attention(proj, mask_bias, bias_tiles, batch, seq, n_heads)
    return matmul_residual(o, w_o.astype(BF16), jnp.zeros((1, d), F32), x)


def conv_layer(x, norm_g, w_pw1, b_pw1, w_dw, b_dw, ln_g, ln_b, w_pw2, b_pw2, seq):
    t, d = x.shape
    u = norm_matmul_glu(x, norm_g.reshape(1, d), w_pw1.astype(BF16), b_pw1.reshape(1, 2 * d))
    w_dw2 = jnp.pad(w_dw[:, 0, :], ((0, CONV_HALO - CONV_WIDTH), (0, 0)))
    v = dwconv_ln_swish(u, w_dw2, b_dw.reshape(1, d), ln_g.reshape(1, d), ln_b.reshape(1, d), seq)
    return matmul_residual(v, w_pw2.astype(BF16), b_pw2.reshape(1, d), x)


def kernel(x, norm_mix, norm_ffn, final_norm, conv_w_pw1, conv_b_pw1, conv_w_dw, conv_b_dw, conv_ln_g,
           conv_ln_b, conv_w_pw2, conv_b_pw2, attn_w_in, attn_w_o, rel_bias, router_w, router_b, moe_w_gu,
           moe_b_gu, moe_w_down, moe_b_down):
    batch, seq, d = x.shape
    depth = norm_mix.shape[0]
    h = x.reshape(batch * seq, d)
    for i in range(depth):
        jdx = i // 2
        if i % 2 == 0:
            h = conv_layer(h, norm_mix[i], conv_w_pw1[jdx], conv_b_pw1[jdx], conv_w_dw[jdx], conv_b_dw[jdx],
                           conv_ln_g[jdx], conv_ln_b[jdx], conv_w_pw2[jdx], conv_b_pw2[jdx], seq)
        else:
            h = attention_layer(h, norm_mix[i], attn_w_in[jdx], attn_w_o[jdx], rel_bias, batch, seq)
        h = moe_layer(h, norm_ffn[i], router_w[i], router_b[i], i, moe_w_gu, moe_b_gu, moe_w_down, moe_b_down,
                      final_g=final_norm if i == depth - 1 else None)
    return h.reshape(batch, seq, d)
```

```python
import functools
import math

import numpy as np
import jax
import jax.numpy as jnp
from jax import lax
from jax.experimental import pallas as pl
from jax.experimental.pallas import tpu as pltpu

F32 = jnp.float32
BF16 = jnp.bfloat16
I32 = jnp.int32

NORM_EPS = 1e-5
CONV_WIDTH = 31
HEAD_DIM = 128
N_KV_HEADS = 4
IDX_HEADS = 16
IDX_DIM = 64
INDEX_TOPK_MAX = 256
NUM_BUCKETS = 32
MAX_DISTANCE = 128
N_EXPERTS = 32
TOP_K = 4
SWIGLU_LIMIT = 7.0
SWIGLU_ALPHA = 1.702
MOE_BLOCK = 256
MOE_ITEM_BLOCKS = 10
MOE_TFA = 256
MOE_TNB = 256
ATT_TILE = 256
CONV_HALO = 32
NEG_BIG = -1e30
INT_MIN = -2147483648
VMEM_LIMIT = 56 * 1024 * 1024


def _cparams(sem):
    return pltpu.CompilerParams(dimension_semantics=sem, vmem_limit_bytes=VMEM_LIMIT)


def _rms(x, g):
    ms = jnp.mean(x * x, axis=-1, keepdims=True)
    return (x * lax.rsqrt(ms + NORM_EPS)) * g


def _sigmoid(x):
    return 1.0 / (1.0 + jnp.exp(-x))


def _norm_mm_kernel(x_ref, g_ref, w_ref, b_ref, s_ref, o_ref, hn_ref):
    @pl.when(pl.program_id(1) == 0)
    def _():
        hn_ref[...] = _rms(x_ref[...], g_ref[...]).astype(BF16)

    acc = jnp.dot(hn_ref[...], w_ref[...], preferred_element_type=F32)
    o_ref[...] = ((acc + b_ref[...]) * s_ref[...]).astype(o_ref.dtype)


def norm_matmul(x, g, w, b, s, out_dtype, tm=512, tn=512):
    t, d = x.shape
    n = w.shape[1]
    tn = min(tn, n)
    return pl.pallas_call(
        _norm_mm_kernel,
        grid=(t // tm, n // tn),
        in_specs=[
            pl.BlockSpec((tm, d), lambda i, j: (i, 0)),
            pl.BlockSpec((1, d), lambda i, j: (0, 0)),
            pl.BlockSpec((d, tn), lambda i, j: (0, j)),
            pl.BlockSpec((1, tn), lambda i, j: (0, j)),
            pl.BlockSpec((1, tn), lambda i, j: (0, j)),
        ],
        out_specs=pl.BlockSpec((tm, tn), lambda i, j: (i, j)),
        out_shape=jax.ShapeDtypeStruct((t, n), out_dtype),
        scratch_shapes=[pltpu.VMEM((tm, d), BF16)],
        compiler_params=_cparams(("parallel", "arbitrary")),
        name="norm_matmul",
    )(x, g, w, b, s)


def _norm_mm_glu_kernel(x_ref, g_ref, wa_ref, wg_ref, ba_ref, bg_ref, o_ref, hn_ref):
    @pl.when(pl.program_id(1) == 0)
    def _():
        hn_ref[...] = _rms(x_ref[...], g_ref[...]).astype(BF16)

    hn = hn_ref[...]
    a = jnp.dot(hn, wa_ref[...], preferred_element_type=F32) + ba_ref[...]
    gt = jnp.dot(hn, wg_ref[...], preferred_element_type=F32) + bg_ref[...]
    o_ref[...] = (a * _sigmoid(gt)).astype(o_ref.dtype)


def norm_matmul_glu(x, g, w, b, tm=512, tn=512):
    t, d = x.shape
    n = w.shape[1] // 2
    nj = n // tn
    return pl.pallas_call(
        _norm_mm_glu_kernel,
        grid=(t // tm, nj),
        in_specs=[
            pl.BlockSpec((tm, d), lambda i, j: (i, 0)),
            pl.BlockSpec((1, d), lambda i, j: (0, 0)),
            pl.BlockSpec((d, tn), lambda i, j: (0, j)),
            pl.BlockSpec((d, tn), lambda i, j: (0, j + nj)),
            pl.BlockSpec((1, tn), lambda i, j: (0, j)),
            pl.BlockSpec((1, tn), lambda i, j: (0, j + nj)),
        ],
        out_specs=pl.BlockSpec((tm, tn), lambda i, j: (i, j)),
        out_shape=jax.ShapeDtypeStruct((t, n), BF16),
        scratch_shapes=[pltpu.VMEM((tm, d), BF16)],
        compiler_params=_cparams(("parallel", "arbitrary")),
        name="norm_pw1_glu",
    )(x, g, w, w, b, b)


def _mm_res_kernel(a_ref, w_ref, b_ref, r_ref, o_ref):
    acc = jnp.dot(a_ref[...], w_ref[...], preferred_element_type=F32)
    o_ref[...] = r_ref[...] + (acc + b_ref[...])


def matmul_residual(a, w, b, res, tm=512, tn=512):
    t, k = a.shape
    n = w.shape[1]
    return pl.pallas_call(
        _mm_res_kernel,
        grid=(t // tm, n // tn),
        in_specs=[
            pl.BlockSpec((tm, k), lambda i, j: (i, 0)),
            pl.BlockSpec((k, tn), lambda i, j: (0, j)),
            pl.BlockSpec((1, tn), lambda i, j: (0, j)),
            pl.BlockSpec((tm, tn), lambda i, j: (i, j)),
        ],
        out_specs=pl.BlockSpec((tm, tn), lambda i, j: (i, j)),
        out_shape=jax.ShapeDtypeStruct((t, n), F32),
        compiler_params=_cparams(("parallel", "parallel")),
        name="matmul_residual",
    )(a, w, b, res)


def _dwconv_kernel(prev_ref, cur_ref, w_ref, bdw_ref, lg_ref, lb_ref, o_ref, buf_ref, conv_ref,
                   *, ts, tiles_per_seq, col_chunk, row_chunk):
    i = pl.program_id(0)
    first = (i % tiles_per_seq) == 0
    prev = prev_ref[...].astype(F32)
    buf_ref[0:CONV_HALO, :] = jnp.where(first, 0.0, prev)
    buf_ref[CONV_HALO:, :] = cur_ref[...].astype(F32)
    d = cur_ref.shape[1]
    shift = CONV_HALO - (CONV_WIDTH - 1)

    def col_body(c, carry):
        col = pl.ds(pl.multiple_of(c * col_chunk, col_chunk), col_chunk)
        for r0 in range(0, ts, row_chunk):
            acc = jnp.broadcast_to(bdw_ref[:, col], (row_chunk, col_chunk))
            for k in range(CONV_WIDTH):
                acc = acc + w_ref[k:k + 1, col] * buf_ref[pl.ds(r0 + k + shift, row_chunk), col]
            conv_ref[pl.ds(r0, row_chunk), col] = acc
        return carry

    lax.fori_loop(0, d // col_chunk, col_body, 0)
    y = conv_ref[...]
    mu = jnp.mean(y, axis=-1, keepdims=True)
    yc = y - mu
    var = jnp.mean(yc * yc, axis=-1, keepdims=True)
    z = (yc * lax.rsqrt(var + NORM_EPS)) * lg_ref[...] + lb_ref[...]
    o_ref[...] = (z * _sigmoid(z)).astype(o_ref.dtype)


def dwconv_ln_swish(u, w_dw, b_dw, ln_g, ln_b, seq, ts=256):
    t, d = u.shape
    hb = ts // CONV_HALO
    kern = functools.partial(_dwconv_kernel, ts=ts, tiles_per_seq=seq // ts, col_chunk=512, row_chunk=64)
    return pl.pallas_call(
        kern,
        grid=(t // ts,),
        in_specs=[
            pl.BlockSpec((CONV_HALO, d), lambda i: (jnp.maximum(i * hb - 1, 0), 0)),
            pl.BlockSpec((ts, d), lambda i: (i, 0)),
            pl.BlockSpec((CONV_HALO, d), lambda i: (0, 0)),
            pl.BlockSpec((1, d), lambda i: (0, 0)),
            pl.BlockSpec((1, d), lambda i: (0, 0)),
            pl.BlockSpec((1, d), lambda i: (0, 0)),
        ],
        out_specs=pl.BlockSpec((ts, d), lambda i: (i, 0)),
        out_shape=jax.ShapeDtypeStruct((t, d), BF16),
        scratch_shapes=[pltpu.VMEM((ts + CONV_HALO, d), F32), pltpu.VMEM((ts, d), F32)],
        compiler_params=_cparams(("parallel",)),
        name="dwconv_ln_swish",
    )(u, u, w_dw, b_dw, ln_g, ln_b)


def _router_kernel(x_ref, g_ref, rw_ref, rb_ref, hn_ref, idx_ref, gate_ref, rank_ref, cnt_ref):
    @pl.when(pl.program_id(0) == 0)
    def _():
        cnt_ref[...] = jnp.zeros(cnt_ref.shape, F32)

    hn = _rms(x_ref[...], g_ref[...])
    hn_ref[...] = hn.astype(BF16)
    logits = jnp.dot(hn, rw_ref[...], preferred_element_type=F32,
                     precision=lax.Precision.HIGHEST) + rb_ref[...]
    tm, ne = logits.shape
    lane = lax.broadcasted_iota(I32, (tm, ne), 1).astype(F32)
    lane4 = lax.broadcasted_iota(I32, (tm, TOP_K), 1)
    work = logits
    vals, hits = [], []
    idx_out = jnp.zeros((tm, TOP_K), F32)
    for k in range(TOP_K):
        m = jnp.max(work, axis=-1, keepdims=True)
        idx = jnp.min(jnp.where(work == m, lane, float(ne)), axis=-1, keepdims=True)
        hit = lane == idx
        vals.append(m)
        hits.append(hit)
        idx_out = jnp.where(lane4 == k, idx, idx_out)
        work = jnp.where(hit, -jnp.inf, work)
    es = [jnp.exp(v - vals[0]) for v in vals]
    denom = es[0] + es[1] + es[2] + es[3]
    sel = jnp.zeros((tm, ne), F32)
    gate_out = jnp.zeros((tm, TOP_K), F32)
    for k in range(TOP_K):
        sel = jnp.where(hits[k], 1.0, sel)
        gate_out = jnp.where(lane4 == k, es[k] / denom, gate_out)
    r_i = lax.broadcasted_iota(I32, (tm, tm), 0)
    c_i = lax.broadcasted_iota(I32, (tm, tm), 1)
    tri = jnp.where(c_i < r_i, 1.0, 0.0).astype(BF16)
    rank_full = jnp.dot(tri, sel.astype(BF16), preferred_element_type=F32) + cnt_ref[...]
    rank_out = jnp.zeros((tm, TOP_K), F32)
    for k in range(TOP_K):
        rk = jnp.sum(jnp.where(hits[k], rank_full, 0.0), axis=-1, keepdims=True)
        rank_out = jnp.where(lane4 == k, rk, rank_out)
    cnt_ref[...] = cnt_ref[...] + jnp.sum(sel, axis=0, keepdims=True)
    idx_ref[...] = idx_out.astype(I32)
    gate_ref[...] = gate_out
    rank_ref[...] = rank_out.astype(I32)


def router(x, g, rw, rb, tm=512):
    t, d = x.shape
    ne = rw.shape[1]
    return pl.pallas_call(
        _router_kernel,
        grid=(t // tm,),
        in_specs=[
            pl.BlockSpec((tm, d), lambda i: (i, 0)),
            pl.BlockSpec((1, d), lambda i: (0, 0)),
            pl.BlockSpec((d, ne), lambda i: (0, 0)),
            pl.BlockSpec((1, ne), lambda i: (0, 0)),
        ],
        out_specs=[
            pl.BlockSpec((tm, d), lambda i: (i, 0)),
            pl.BlockSpec((tm, TOP_K), lambda i: (i, 0)),
            pl.BlockSpec((tm, TOP_K), lambda i: (i, 0)),
            pl.BlockSpec((tm, TOP_K), lambda i: (i, 0)),
            pl.BlockSpec((1, ne), lambda i: (0, 0)),
        ],
        out_shape=[
            jax.ShapeDtypeStruct((t, d), BF16),
            jax.ShapeDtypeStruct((t, TOP_K), I32),
            jax.ShapeDtypeStruct((t, TOP_K), F32),
            jax.ShapeDtypeStruct((t, TOP_K), I32),
            jax.ShapeDtypeStruct((1, ne), F32),
        ],
        compiler_params=_cparams(("arbitrary",)),
        name="moe_router",
    )(x, g, rw, rb)


def _moe_kernel(ie_ref, ib_ref, in_ref, x_ref, wg_ref, wu_ref, wd_ref, bg_ref, bu_ref, bd_ref,
                o_ref, act_ref, *, nfa):
    w = pl.program_id(0)
    j = pl.program_id(1)
    nblk = in_ref[w]
    group = 4
    group_rows = group * MOE_BLOCK
    ngroup = nblk // group

    def for_row_groups(fn):
        def body(p, carry):
            fn(pl.multiple_of(p * group_rows, group_rows), group_rows)
            return carry

        lax.fori_loop(0, ngroup, body, 0)
        for rem in range(1, group):
            @pl.when(nblk % group == rem)
            def _():
                fn(pl.multiple_of(ngroup * group_rows, group_rows), rem * MOE_BLOCK)

    @pl.when((nblk > 0) & (j < nfa))
    def _():
        wg = wg_ref[0, 0].astype(BF16)
        wu = wu_ref[0, 0].astype(BF16)
        bg = bg_ref[0, 0]
        bu = bu_ref[0, 0]
        cols = pl.ds(pl.multiple_of(j * MOE_TFA, MOE_TFA), MOE_TFA)

        def gate_up(r0, nrows):
            xr = x_ref[pl.ds(r0, nrows), :]
            g = jnp.dot(xr, wg, preferred_element_type=F32) + bg
            u = jnp.dot(xr, wu, preferred_element_type=F32) + bu
            g = jnp.minimum(g, SWIGLU_LIMIT)
            u = jnp.clip(u, -SWIGLU_LIMIT, SWIGLU_LIMIT)
            act = (u + 1.0) * (g * _sigmoid(g * SWIGLU_ALPHA))
            act_ref[pl.ds(r0, nrows), cols] = act.astype(BF16)

        for_row_groups(gate_up)

    @pl.when((nblk > 0) & (j >= nfa))
    def _():
        wd = wd_ref[0, 0].astype(BF16)
        bd = bd_ref[0, 0]

        def down(r0, nrows):
            y = jnp.dot(act_ref[pl.ds(r0, nrows), :], wd, preferred_element_type=F32) + bd
            o_ref[pl.ds(r0, nrows), :] = y.astype(o_ref.dtype)

        for_row_groups(down)


def moe_experts(xs, item_expert, item_blk0, item_nblk, layer, w_gu, b_gu, w_down, b_down):
    p_rows, d = xs.shape
    depth, ne, _, dff2 = w_gu.shape
    dff = dff2 // 2
    nfa = dff // MOE_TFA
    nfb = d // MOE_TNB
    n_items = item_expert.shape[0]
    item_rows = MOE_ITEM_BLOCKS * MOE_BLOCK

    def ja(w, j, inb):
        return jnp.where(inb[w] > 0, jnp.minimum(j, nfa - 1), nfa - 1)

    def jb(w, j, inb):
        return jnp.where(inb[w] > 0, jnp.maximum(j - nfa, 0), nfb - 1)

    b_gu4 = b_gu.reshape(depth, ne, 1, dff2)
    b_down4 = b_down.reshape(depth, ne, 1, d)
    grid_spec = pltpu.PrefetchScalarGridSpec(
        num_scalar_prefetch=3,
        grid=(n_items, nfa + nfb),
        in_specs=[
            pl.BlockSpec((pl.Element(item_rows), pl.Element(d)),
                         lambda w, j, ie, ib, inb: (ib[w] * MOE_BLOCK, 0)),
            pl.BlockSpec((1, 1, d, MOE_TFA), lambda w, j, ie, ib, inb: (layer, ie[w], 0, ja(w, j, inb))),
            pl.BlockSpec((1, 1, d, MOE_TFA), lambda w, j, ie, ib, inb: (layer, ie[w], 0, nfa + ja(w, j, inb))),
            pl.BlockSpec((1, 1, dff, MOE_TNB), lambda w, j, ie, ib, inb: (layer, ie[w], 0, jb(w, j, inb))),
            pl.BlockSpec((1, 1, 1, MOE_TFA), lambda w, j, ie, ib, inb: (layer, ie[w], 0, ja(w, j, inb))),
            pl.BlockSpec((1, 1, 1, MOE_TFA), lambda w, j, ie, ib, inb: (layer, ie[w], 0, nfa + ja(w, j, inb))),
            pl.BlockSpec((1, 1, 1, MOE_TNB), lambda w, j, ie, ib, inb: (layer, ie[w], 0, jb(w, j, inb))),
        ],
        out_specs=pl.BlockSpec((item_rows, MOE_TNB), lambda w, j, ie, ib, inb: (w, jb(w, j, inb))),
        scratch_shapes=[pltpu.VMEM((item_rows, dff), BF16)],
    )
    return pl.pallas_call(
        functools.partial(_moe_kernel, nfa=nfa),
        grid_spec=grid_spec,
        out_shape=jax.ShapeDtypeStruct((n_items * item_rows, d), BF16),
        compiler_params=_cparams(("arbitrary", "arbitrary")),
        name="moe_experts",
    )(item_expert, item_blk0, item_nblk, xs, w_gu, w_gu, w_down, b_gu4, b_gu4, b_down4)


def _combine_kernel(*refs, final):
    y_refs = refs[:TOP_K]
    g_ref, x_ref = refs[TOP_K], refs[TOP_K + 1]
    o_ref = refs[-1]
    g = g_ref[...]
    acc = x_ref[...]
    for k in range(TOP_K):
        acc = acc + g[:, k:k + 1] * y_refs[k][...].astype(F32)
    if final:
        acc = _rms(acc, refs[TOP_K + 2][...])
    o_ref[...] = acc


def moe_combine(yk, gate4, x, final_g=None, tm=256):
    t, d = x.shape
    final = final_g is not None
    nt = t // tm

    def y_map(k):
        return lambda i: (k * nt + i, 0)

    in_specs = [pl.BlockSpec((tm, d), y_map(k)) for k in range(TOP_K)] + [
        pl.BlockSpec((tm, TOP_K), lambda i: (i, 0)),
        pl.BlockSpec((tm, d), lambda i: (i, 0)),
    ]
    args = [yk] * TOP_K + [gate4, x]
    if final:
        in_specs.append(pl.BlockSpec((1, d), lambda i: (0, 0)))
        args.append(final_g.reshape(1, d))
    return pl.pallas_call(
        functools.partial(_combine_kernel, final=final),
        grid=(t // tm,),
        in_specs=in_specs,
        out_specs=pl.BlockSpec((tm, d), lambda i: (i, 0)),
        out_shape=jax.ShapeDtypeStruct((t, d), F32),
        compiler_params=_cparams(("parallel",)),
        name="moe_combine",
    )(*args)


def _lookup(table, idx):
    onehot = idx[..., None] == jnp.arange(table.shape[0], dtype=I32)
    return jnp.sum(jnp.where(onehot, table, 0), axis=-1)


def moe_layer(x, norm_g, router_w, router_b, layer, w_gu, b_gu, w_down, b_down, final_g=None):
    t, d = x.shape
    ne = router_w.shape[1]
    hn, top_idx, gate4, rank4, cnt = router(x, norm_g.reshape(1, d), router_w, router_b.reshape(1, ne))

    n_assign = t * TOP_K
    n_blocks = -(-n_assign // MOE_BLOCK) + ne
    p_rows = n_blocks * MOE_BLOCK
    item_rows = MOE_ITEM_BLOCKS * MOE_BLOCK
    counts = cnt[0].astype(I32)
    nb = (counts + MOE_BLOCK - 1) // MOE_BLOCK
    blk_start = jnp.cumsum(nb) - nb

    max_items = n_blocks // MOE_ITEM_BLOCKS + ne
    n_it = (nb + MOE_ITEM_BLOCKS - 1) // MOE_ITEM_BLOCKS
    it_cum = jnp.cumsum(n_it)
    it_start = it_cum - n_it
    total_items = it_cum[-1]
    base_e = nb // jnp.maximum(n_it, 1)
    rem_e = nb % jnp.maximum(n_it, 1)
    wids = jnp.arange(max_items, dtype=I32)
    e_of = jnp.minimum(jnp.searchsorted(it_cum, wids, side="right"), ne - 1).astype(I32)
    local = wids - it_start[e_of]
    size = base_e[e_of] + (local < rem_e[e_of]).astype(I32)
    off = local * base_e[e_of] + jnp.minimum(local, rem_e[e_of])
    valid = wids < total_items
    last_e = e_of[jnp.maximum(total_items - 1, 0)]
    item_expert = jnp.where(valid, e_of, last_e).astype(I32)
    item_nblk = jnp.where(valid, size, 0).astype(I32)
    item_blk0 = jnp.where(valid, blk_start[e_of] + off, 0).astype(I32)

    a_e = top_idx.T.reshape(-1)
    a_rank = rank4.T.reshape(-1)
    blk_in_e = a_rank // MOE_BLOCK
    within = a_rank % MOE_BLOCK
    a_base = _lookup(base_e, a_e)
    a_rem = _lookup(rem_e, a_e)
    dest = (_lookup(blk_start, a_e) + blk_in_e) * MOE_BLOCK + within
    big = a_base + 1
    n_big = a_rem * big
    in_big = blk_in_e < n_big
    num = jnp.where(in_big, blk_in_e, blk_in_e - n_big)
    den = jnp.where(in_big, big, jnp.maximum(a_base, 1))
    quo = jnp.floor((num.astype(F32) + 0.5) / den.astype(F32)).astype(I32)
    a_local = jnp.where(in_big, 0, a_rem) + quo
    a_slot = num - quo * den
    out_row = ((_lookup(it_start, a_e) + a_local) * MOE_ITEM_BLOCKS + a_slot) * MOE_BLOCK + within

    n_rows = p_rows + item_rows
    row_token = (jnp.arange(n_rows, dtype=I32) % t).at[dest].set(jnp.tile(jnp.arange(t, dtype=I32), TOP_K))
    xs = hn[row_token]
    ys = moe_experts(xs, item_expert, item_blk0, item_nblk, layer, w_gu, b_gu, w_down, b_down)
    yk = ys[out_row]
    return moe_combine(yk, gate4, x, final_g)


def _indexer_kernel(qi_ref, kit_ref, wi_ref, o_ref, key_ref, wb_ref, cnt_ref, *, tq, topk):
    i = pl.program_id(1)
    nch = i + 1
    o_ref[...] = jnp.full(o_ref.shape, NEG_BIG, o_ref.dtype)
    row = lax.broadcasted_iota(I32, (tq, tq), 0)
    col = lax.broadcasted_iota(I32, (tq, tq), 1)
    wi = wi_ref[...]
    for h in range(IDX_HEADS):
        wb_ref[h] = jnp.broadcast_to(wi[:, h:h + 1], (tq, 128))

    def chunk_slice(c):
        return pl.ds(pl.multiple_of(c * tq, tq), tq)

    def score_chunk(c, carry):
        cs = chunk_slice(c)
        kc = kit_ref[0, :, cs]
        acc = jnp.zeros((tq, tq), F32)
        for h in range(IDX_HEADS):
            s = jnp.dot(qi_ref[:, h * IDX_DIM:(h + 1) * IDX_DIM], kc, preferred_element_type=F32)
            wh = wb_ref[h]
            acc = acc + jnp.concatenate([wh] * (tq // 128), axis=1) * jnp.maximum(s, 0.0)
        bits = pltpu.bitcast(acc, I32)
        key = bits ^ ((bits >> 31) & 0x7FFFFFFF)
        key = jnp.where((c < i) | (col <= row), key, INT_MIN)
        key_ref[:, cs] = key
        return carry

    lax.fori_loop(0, nch, score_chunk, 0)

    pos = i * tq + lax.broadcasted_iota(I32, (tq, 128), 0)
    kk = jnp.minimum(pos + 1, topk).astype(F32)

    @pl.when(nch % 2 == 1)
    def _():
        key_ref[:, chunk_slice(nch)] = jnp.full((tq, tq), INT_MIN, I32)

    cnt_rows = 64
    cnt_cols = 2 * tq

    ones = jnp.ones((128, 128), BF16)

    def body(it, v):
        cand = v | jnp.left_shift(jnp.int32(1), 31 - it)
        thr = cand ^ INT_MIN
        for r0 in range(0, tq, cnt_rows):
            thr_r = thr[r0:r0 + cnt_rows]

            def cnt_step(c, acc):
                kch = key_ref[r0:r0 + cnt_rows, pl.ds(pl.multiple_of(c * cnt_cols, cnt_cols), cnt_cols)]
                for q in range(cnt_cols // 128):
                    acc = acc + jnp.where(kch[:, q * 128:(q + 1) * 128] >= thr_r, 1.0, 0.0)
                return acc

            cnt_ref[r0:r0 + cnt_rows, :] = lax.fori_loop(0, (nch + 1) // 2, cnt_step,
                                                         jnp.zeros((cnt_rows, 128), F32))
        cnt = jnp.dot(cnt_ref[...].astype(BF16), ones, preferred_element_type=F32)
        return jnp.where(cnt >= kk, cand, v)

    v = lax.fori_loop(0, 32, body, jnp.zeros((tq, 128), I32))
    thr = jnp.concatenate([v ^ INT_MIN] * (tq // 128), axis=1)

    def out_chunk(c, carry):
        cs = chunk_slice(c)
        o_ref[:, cs] = jnp.where(key_ref[:, cs] >= thr, 0.0, NEG_BIG).astype(o_ref.dtype)
        return carry

    lax.fori_loop(0, nch, out_chunk, 0)


def indexer_mask(proj, kit, wi, batch, seq, qi_col_block):
    t = proj.shape[0]
    tq = ATT_TILE
    nq = seq // tq
    qi_cols = IDX_HEADS * IDX_DIM
    topk = min(INDEX_TOPK_MAX, seq // 4)
    kern = functools.partial(_indexer_kernel, tq=tq, topk=topk)
    return pl.pallas_call(
        kern,
        grid=(batch, nq),
        in_specs=[
            pl.BlockSpec((tq, qi_cols), lambda b, i: (b * nq + i, qi_col_block)),
            pl.BlockSpec((1, IDX_DIM, seq), lambda b, i: (b, 0, 0)),
            pl.BlockSpec((tq, IDX_HEADS), lambda b, i: (b * nq + i, 0)),
        ],
        out_specs=pl.BlockSpec((tq, seq), lambda b, i: (b * nq + i, 0)),
        out_shape=jax.ShapeDtypeStruct((t, seq), BF16),
        scratch_shapes=[pltpu.VMEM((tq, seq), I32), pltpu.VMEM((IDX_HEADS, tq, 128), F32),
                        pltpu.VMEM((tq, 128), F32)],
        compiler_params=_cparams(("parallel", "parallel")),
        name="dsa_indexer",
    )(proj, kit, wi)


def _attn_kernel(qt_ref, kt_ref, q_ref, k_ref, v_ref, mb_ref, bias_ref, o_ref, acc_ref, m_ref, l_ref,
                 *, n_heads, group):
    i = qt_ref[pl.program_id(1)]
    j = kt_ref[pl.program_id(1)]
    hd = HEAD_DIM
    tq, tk = mb_ref.shape

    @pl.when(j == 0)
    def _():
        acc_ref[...] = jnp.zeros(acc_ref.shape, F32)
        m_ref[...] = jnp.full(m_ref.shape, NEG_BIG, F32)
        l_ref[...] = jnp.zeros(l_ref.shape, F32)

    def heads(near):
        mb = mb_ref[...].astype(F32)
        off = i - j
        for h in range(n_heads):
            n = h // group
            qh = q_ref[:, h * hd:(h + 1) * hd]
            kn = k_ref[:, n * hd:(n + 1) * hd]
            s = lax.dot_general(qh, kn, (((1,), (1,)), ((), ())), preferred_element_type=F32)
            if near:
                s = s + bias_ref[off, h]
            s = s + mb
            m_prev = m_ref[h]
            m_cur = jnp.max(s, axis=1, keepdims=True)
            m_next = jnp.maximum(m_prev, m_cur)
            alpha = jnp.exp(m_prev - m_next)
            p = jnp.exp(s - jnp.concatenate([m_next] * (tk // 128), axis=1))
            l_ref[h] = alpha * l_ref[h] + jnp.sum(p, axis=1, keepdims=True)
            m_ref[h] = m_next
            pv = jnp.dot(p.astype(BF16), v_ref[:, n * hd:(n + 1) * hd], preferred_element_type=F32)
            acc_ref[:, h * hd:(h + 1) * hd] = acc_ref[:, h * hd:(h + 1) * hd] * alpha + pv

    @pl.when(i - j < 2)
    def _():
        heads(True)

    @pl.when(i - j >= 2)
    def _():
        heads(False)

    @pl.when(j == i)
    def _():
        for h in range(n_heads):
            o_ref[:, h * hd:(h + 1) * hd] = (acc_ref[:, h * hd:(h + 1) * hd] / l_ref[h]).astype(o_ref.dtype)


def sparse_attention(proj, mask_bias, bias_tiles, batch, seq, n_heads):
    t = proj.shape[0]
    tq = tk = ATT_TILE
    nq = seq // tq
    q_cols = n_heads * HEAD_DIM
    kv_cols = N_KV_HEADS * HEAD_DIM
    k_blk = q_cols // kv_cols
    kern = functools.partial(_attn_kernel, n_heads=n_heads, group=n_heads // N_KV_HEADS)
    pairs = [(i, j) for i in range(nq) for j in range(i + 1)]
    q_tile = jnp.asarray([p[0] for p in pairs], I32)
    k_tile = jnp.asarray([p[1] for p in pairs], I32)
    grid_spec = pltpu.PrefetchScalarGridSpec(
        num_scalar_prefetch=2,
        grid=(batch, len(pairs)),
        in_specs=[
            pl.BlockSpec((tq, q_cols), lambda b, p, qt, kt: (b * nq + qt[p], 0)),
            pl.BlockSpec((tk, kv_cols), lambda b, p, qt, kt: (b * nq + kt[p], k_blk)),
            pl.BlockSpec((tk, kv_cols), lambda b, p, qt, kt: (b * nq + kt[p], k_blk + 1)),
            pl.BlockSpec((tq, tk), lambda b, p, qt, kt: (b * nq + qt[p], kt[p])),
            pl.BlockSpec((2, n_heads, tq, tk), lambda b, p, qt, kt: (0, 0, 0, 0)),
        ],
        out_specs=pl.BlockSpec((tq, q_cols), lambda b, p, qt, kt: (b * nq + qt[p], 0)),
        scratch_shapes=[
            pltpu.VMEM((tq, q_cols), F32),
            pltpu.VMEM((n_heads, tq, 128), F32),
            pltpu.VMEM((n_heads, tq, 128), F32),
        ],
    )
    return pl.pallas_call(
        kern,
        grid_spec=grid_spec,
        out_shape=jax.ShapeDtypeStruct((t, q_cols), BF16),
        compiler_params=_cparams(("parallel", "arbitrary")),
        name="dsa_attention",
    )(q_tile, k_tile, proj, proj, proj, mask_bias, bias_tiles)


def _t5_bucket(n):
    n = jnp.maximum(n, 0)
    max_exact = NUM_BUCKETS // 2
    nf = jnp.maximum(n, 1).astype(F32)
    large = max_exact + (jnp.log(nf / max_exact) / math.log(MAX_DISTANCE / max_exact)
                         * (NUM_BUCKETS - max_exact)).astype(I32)
    large = jnp.minimum(large, NUM_BUCKETS - 1)
    return jnp.where(n < max_exact, n, large)


def _toeplitz(by_delta, n):
    h = by_delta.shape[0]
    u = jnp.concatenate([by_delta[:, :n][:, ::-1], jnp.zeros((h, 2), by_delta.dtype),
                         by_delta[:, n:][:, ::-1]], axis=1)
    rows = jnp.tile(u, (1, n))[:, :n * 2 * n].reshape(h, n, 2 * n)
    return rows[:, :, :n]


def attention_layer(x, norm_g, w_in, w_o, rel_bias, batch, seq):
    t, d = x.shape
    n_heads = w_o.shape[0] // HEAD_DIM
    q_cols = n_heads * HEAD_DIM
    kv_cols = N_KV_HEADS * HEAD_DIM
    qi_cols = IDX_HEADS * IDX_DIM
    main_cols = q_cols + 2 * kv_cols + qi_cols
    assert main_cols % qi_cols == 0 and q_cols % kv_cols == 0
    tail_cols = 128
    w_main = w_in[:, :main_cols].astype(BF16)
    w_tail = jnp.pad(w_in[:, main_cols:], ((0, 0), (0, tail_cols - (IDX_DIM + IDX_HEADS)))).astype(BF16)
    s_main = jnp.concatenate([jnp.full((q_cols,), HEAD_DIM ** -0.5, F32),
                              jnp.ones((main_cols - q_cols,), F32)]).reshape(1, main_cols)
    s_tail = jnp.concatenate([jnp.ones((IDX_DIM,), F32),
                              jnp.full((IDX_HEADS,), (IDX_HEADS ** -0.5) * (IDX_DIM ** -0.5), F32),
                              jnp.ones((tail_cols - IDX_DIM - IDX_HEADS,), F32)]).reshape(1, tail_cols)
    g2 = norm_g.reshape(1, d)
    proj = norm_matmul(x, g2, w_main, jnp.zeros((1, main_cols), F32), s_main, BF16)
    tail = norm_matmul(x, g2, w_tail, jnp.zeros((1, tail_cols), F32), s_tail, F32)
    kit = tail[:, :IDX_DIM].astype(BF16).reshape(batch, seq, IDX_DIM).transpose(0, 2, 1)
    wi = tail[:, IDX_DIM:IDX_DIM + IDX_HEADS]
    mask_bias = indexer_mask(proj, kit, wi, batch, seq, (q_cols + 2 * kv_cols) // qi_cols)

    tile = ATT_TILE
    assert tile >= MAX_DISTANCE
    dist = jnp.arange(2 * tile, dtype=I32)
    onehot = (_t5_bucket(dist)[:, None] == jnp.arange(NUM_BUCKETS, dtype=I32)).astype(F32)
    tbl = jnp.dot(onehot, rel_bias - rel_bias[NUM_BUCKETS - 1][None, :], precision=lax.Precision.HIGHEST)
    tbl = jnp.where((dist >= MAX_DISTANCE)[:, None], 0.0, tbl).T
    by_delta0 = jnp.concatenate([jnp.zeros((n_heads, tile - 1), F32), tbl[:, :tile]], axis=1)
    by_delta1 = tbl[:, 1:]
    bias_tiles = jnp.stack([_toeplitz(by_delta0, tile), _toeplitz(by_delta1, tile)], axis=0)

    o = sparse_attention(proj, mask_bias, bias_tiles, batch, seq, n_heads)
    return matmul_residual(o, w_o.astype(BF16), jnp.zeros((1, d), F32), x)


def conv_layer(x, norm_g, w_pw1, b_pw1, w_dw, b_dw, ln_g, ln_b, w_pw2, b_pw2, seq):
    t, d = x.shape
    u = norm_matmul_glu(x, norm_g.reshape(1, d), w_pw1.astype(BF16), b_pw1.reshape(1, 2 * d))
    w_dw2 = jnp.pad(w_dw[:, 0, :], ((0, CONV_HALO - CONV_WIDTH), (0, 0)))
    v = dwconv_ln_swish(u, w_dw2, b_dw.reshape(1, d), ln_g.reshape(1, d), ln_b.reshape(1, d), seq)
    return matmul_residual(v, w_pw2.astype(BF16), b_pw2.reshape(1, d), x)


def kernel(x, norm_mix, norm_ffn, final_norm, conv_w_pw1, conv_b_pw1, conv_w_dw, conv_b_dw, conv_ln_g,
           conv_ln_b, conv_w_pw2, conv_b_pw2, attn_w_in, attn_w_o, rel_bias, router_w, router_b, moe_w_gu,
           moe_b_gu, moe_w_down, moe_b_down):
    batch, seq, d = x.shape
    depth = norm_mix.shape[0]
    h = x.reshape(batch * seq, d)
    for i in range(depth):
        jdx = i // 2
        if i % 2 == 0:
            h = conv_layer(h, norm_mix[i], conv_w_pw1[jdx], conv_b_pw1[jdx], conv_w_dw[jdx], conv_b_dw[jdx],
                           conv_ln_g[jdx], conv_ln_b[jdx], conv_w_pw2[jdx], conv_b_pw2[jdx], seq)
        else:
            h = attention_layer(h, norm_mix[i], attn_w_in[jdx], attn_w_o[jdx], rel_bias, batch, seq)
        h = moe_layer(h, norm_ffn[i], router_w[i], router_b[i], i, moe_w_gu, moe_b_gu, moe_w_down, moe_b_down,
                      final_g=final_norm if i == depth - 1 else None)
    return h.reshape(batch, seq, d)
```

```python
import functools
import math

import numpy as np
import jax
import jax.numpy as jnp
from jax import lax
from jax.experimental import pallas as pl
from jax.experimental.pallas import tpu as pltpu

F32 = jnp.float32
BF16 = jnp.bfloat16
I32 = jnp.int32

NORM_EPS = 1e-5
CONV_WIDTH = 31
HEAD_DIM = 128
N_KV_HEADS = 4
IDX_HEADS = 16
IDX_DIM = 64
INDEX_TOPK_MAX = 256
NUM_BUCKETS = 32
MAX_DISTANCE = 128
N_EXPERTS = 32
TOP_K = 4
SWIGLU_LIMIT = 7.0
SWIGLU_ALPHA = 1.702
MOE_BLOCK = 256
MOE_ITEM_BLOCKS = 10
MOE_TFA = 256
MOE_TNB = 256
ATT_TILE = 256
CONV_HALO = 32
CONV_FOLD = 8
NEG_BIG = -1e30
INT_MIN = -2147483648
LOG2E = math.log2(math.e)
VMEM_LIMIT = 56 * 1024 * 1024


def _cparams(sem):
    return pltpu.CompilerParams(dimension_semantics=sem, vmem_limit_bytes=VMEM_LIMIT)


def _rms(x, g):
    ms = jnp.mean(x * x, axis=-1, keepdims=True)
    return (x * lax.rsqrt(ms + NORM_EPS)) * g


def _sigmoid(x):
    return 1.0 / (1.0 + jnp.exp(-x))


def _norm_mm_kernel(x_ref, g_ref, w_ref, b_ref, s_ref, o_ref, hn_ref):
    @pl.when(pl.program_id(1) == 0)
    def _():
        hn_ref[...] = _rms(x_ref[...], g_ref[...]).astype(BF16)

    acc = jnp.dot(hn_ref[...], w_ref[...], preferred_element_type=F32)
    o_ref[...] = ((acc + b_ref[...]) * s_ref[...]).astype(o_ref.dtype)


def norm_matmul(x, g, w, b, s, out_dtype, tm=1024, tn=512):
    t, d = x.shape
    n = w.shape[1]
    tn = min(tn, n)
    return pl.pallas_call(
        _norm_mm_kernel,
        grid=(t // tm, n // tn),
        in_specs=[
            pl.BlockSpec((tm, d), lambda i, j: (i, 0)),
            pl.BlockSpec((1, d), lambda i, j: (0, 0)),
            pl.BlockSpec((d, tn), lambda i, j: (0, j)),
            pl.BlockSpec((1, tn), lambda i, j: (0, j)),
            pl.BlockSpec((1, tn), lambda i, j: (0, j)),
        ],
        out_specs=pl.BlockSpec((tm, tn), lambda i, j: (i, j)),
        out_shape=jax.ShapeDtypeStruct((t, n), out_dtype),
        scratch_shapes=[pltpu.VMEM((tm, d), BF16)],
        compiler_params=_cparams(("parallel", "arbitrary")),
        name="norm_matmul",
    )(x, g, w, b, s)


def _norm_mm_glu_kernel(x_ref, g_ref, wa_ref, wg_ref, ba_ref, bg_ref, o_ref, hn_ref):
    @pl.when(pl.program_id(1) == 0)
    def _():
        hn_ref[...] = _rms(x_ref[...], g_ref[...]).astype(BF16)

    hn = hn_ref[...]
    a = jnp.dot(hn, wa_ref[...], preferred_element_type=F32) + ba_ref[...]
    gt = jnp.dot(hn, wg_ref[...], preferred_element_type=F32) + bg_ref[...]
    o_ref[...] = (a * _sigmoid(gt)).astype(o_ref.dtype)


def norm_matmul_glu(x, g, w, b, tm=1024, tn=512):
    t, d = x.shape
    n = w.shape[1] // 2
    nj = n // tn
    return pl.pallas_call(
        _norm_mm_glu_kernel,
        grid=(t // tm, nj),
        in_specs=[
            pl.BlockSpec((tm, d), lambda i, j: (i, 0)),
            pl.BlockSpec((1, d), lambda i, j: (0, 0)),
            pl.BlockSpec((d, tn), lambda i, j: (0, j)),
            pl.BlockSpec((d, tn), lambda i, j: (0, j + nj)),
            pl.BlockSpec((1, tn), lambda i, j: (0, j)),
            pl.BlockSpec((1, tn), lambda i, j: (0, j + nj)),
        ],
        out_specs=pl.BlockSpec((tm, tn), lambda i, j: (i, j)),
        out_shape=jax.ShapeDtypeStruct((t, n), F32),
        scratch_shapes=[pltpu.VMEM((tm, d), BF16)],
        compiler_params=_cparams(("parallel", "arbitrary")),
        name="norm_pw1_glu",
    )(x, g, w, w, b, b)


def _mm_res_kernel(a_ref, w_ref, b_ref, r_ref, o_ref):
    acc = jnp.dot(a_ref[...], w_ref[...], preferred_element_type=F32)
    o_ref[...] = r_ref[...] + (acc + b_ref[...])


def matmul_residual(a, w, b, res, tm=512, tn=512):
    t, k = a.shape
    n = w.shape[1]
    return pl.pallas_call(
        _mm_res_kernel,
        grid=(t // tm, n // tn),
        in_specs=[
            pl.BlockSpec((tm, k), lambda i, j: (i, 0)),
            pl.BlockSpec((k, tn), lambda i, j: (0, j)),
            pl.BlockSpec((1, tn), lambda i, j: (0, j)),
            pl.BlockSpec((tm, tn), lambda i, j: (i, j)),
        ],
        out_specs=pl.BlockSpec((tm, tn), lambda i, j: (i, j)),
        out_shape=jax.ShapeDtypeStruct((t, n), F32),
        compiler_params=_cparams(("parallel", "parallel")),
        name="matmul_residual",
    )(a, w, b, res)


def _dwconv_kernel(prev_ref, cur_ref, w_ref, bdw_ref, lg_ref, lb_ref, o_ref, buf_ref,
                   *, ts, tiles_per_seq, row_chunk, norm_chunk):
    i = pl.program_id(0)
    first = (i % tiles_per_seq) == 0
    buf_ref[0:CONV_HALO] = jnp.where(first, 0.0, prev_ref[...])
    buf_ref[CONV_HALO:] = cur_ref[...]
    _, cs, cl = cur_ref.shape
    inv_n = 1.0 / (cs * cl)
    shift = CONV_HALO - (CONV_WIDTH - 1)

    def row_body(rc, carry):
        r0 = pl.multiple_of(rc * row_chunk, row_chunk)
        parts = [jnp.broadcast_to(bdw_ref[...], (row_chunk, cs, cl)), jnp.zeros((row_chunk, cs, cl), F32)]
        for k in range(CONV_WIDTH):
            parts[k % 2] = parts[k % 2] + w_ref[k] * buf_ref[pl.ds(r0 + k + shift, row_chunk)]
        o_ref[pl.ds(r0, row_chunk)] = parts[0] + parts[1]
        return carry

    lax.fori_loop(0, ts // row_chunk, row_body, 0)

    def norm_body(rc, carry):
        rows = pl.ds(pl.multiple_of(rc * norm_chunk, norm_chunk), norm_chunk)
        y = o_ref[rows]
        mu = jnp.sum(y, axis=(1, 2), keepdims=True) * inv_n
        yc = y - mu
        var = jnp.sum(yc * yc, axis=(1, 2), keepdims=True) * inv_n
        z = (yc * lax.rsqrt(var + NORM_EPS)) * lg_ref[...] + lb_ref[...]
        o_ref[rows] = z * _sigmoid(z)
        return carry

    lax.fori_loop(0, ts // norm_chunk, norm_body, 0)


def dwconv_ln_swish(u, w_dw, b_dw, ln_g, ln_b, seq, ts=256):
    t, cs, cl = u.shape
    hb = ts // CONV_HALO
    kern = functools.partial(_dwconv_kernel, ts=ts, tiles_per_seq=seq // ts, row_chunk=8, norm_chunk=32)
    vec = pl.BlockSpec((1, cs, cl), lambda i: (0, 0, 0))
    return pl.pallas_call(
        kern,
        grid=(t // ts,),
        in_specs=[
            pl.BlockSpec((CONV_HALO, cs, cl), lambda i: (jnp.maximum(i * hb - 1, 0), 0, 0)),
            pl.BlockSpec((ts, cs, cl), lambda i: (i, 0, 0)),
            pl.BlockSpec((CONV_HALO, cs, cl), lambda i: (0, 0, 0)),
            vec, vec, vec,
        ],
        out_specs=pl.BlockSpec((ts, cs, cl), lambda i: (i, 0, 0)),
        out_shape=jax.ShapeDtypeStruct((t, cs, cl), F32),
        scratch_shapes=[pltpu.VMEM((ts + CONV_HALO, cs, cl), F32)],
        compiler_params=_cparams(("parallel",)),
        name="dwconv_ln_swish",
    )(u, u, w_dw, b_dw, ln_g, ln_b)


def _router_kernel(x_ref, g_ref, rw_ref, rb_ref, hn_ref, idx_ref, gate_ref, rank_ref, cnt_ref):
    @pl.when(pl.program_id(0) == 0)
    def _():
        cnt_ref[...] = jnp.zeros(cnt_ref.shape, F32)

    hn = _rms(x_ref[...], g_ref[...])
    hn_ref[...] = hn.astype(BF16)
    logits = jnp.dot(hn, rw_ref[...], preferred_element_type=F32,
                     precision=lax.Precision.HIGHEST) + rb_ref[...]
    tm, ne = logits.shape
    lane = lax.broadcasted_iota(I32, (tm, ne), 1).astype(F32)
    lane4 = lax.broadcasted_iota(I32, (tm, TOP_K), 1)
    work = logits
    vals, hits = [], []
    idx_out = jnp.zeros((tm, TOP_K), F32)
    for k in range(TOP_K):
        m = jnp.max(work, axis=-1, keepdims=True)
        idx = jnp.min(jnp.where(work == m, lane, float(ne)), axis=-1, keepdims=True)
        hit = lane == idx
        vals.append(m)
        hits.append(hit)
        idx_out = jnp.where(lane4 == k, idx, idx_out)
        work = jnp.where(hit, -jnp.inf, work)
    es = [jnp.exp(v - vals[0]) for v in vals]
    denom = es[0] + es[1] + es[2] + es[3]
    sel = jnp.zeros((tm, ne), F32)
    gate_out = jnp.zeros((tm, TOP_K), F32)
    for k in range(TOP_K):
        sel = jnp.where(hits[k], 1.0, sel)
        gate_out = jnp.where(lane4 == k, es[k] / denom, gate_out)
    r_i = lax.broadcasted_iota(I32, (tm, tm), 0)
    c_i = lax.broadcasted_iota(I32, (tm, tm), 1)
    tri = jnp.where(c_i < r_i, 1.0, 0.0).astype(BF16)
    rank_full = jnp.dot(tri, sel.astype(BF16), preferred_element_type=F32) + cnt_ref[...]
    rank_out = jnp.zeros((tm, TOP_K), F32)
    for k in range(TOP_K):
        rk = jnp.sum(jnp.where(hits[k], rank_full, 0.0), axis=-1, keepdims=True)
        rank_out = jnp.where(lane4 == k, rk, rank_out)
    cnt_ref[...] = cnt_ref[...] + jnp.sum(sel, axis=0, keepdims=True)
    idx_ref[...] = idx_out.astype(I32)
    gate_ref[...] = gate_out
    rank_ref[...] = rank_out.astype(I32)


def router(x, g, rw, rb, tm=512):
    t, d = x.shape
    ne = rw.shape[1]
    return pl.pallas_call(
        _router_kernel,
        grid=(t // tm,),
        in_specs=[
            pl.BlockSpec((tm, d), lambda i: (i, 0)),
            pl.BlockSpec((1, d), lambda i: (0, 0)),
            pl.BlockSpec((d, ne), lambda i: (0, 0)),
            pl.BlockSpec((1, ne), lambda i: (0, 0)),
        ],
        out_specs=[
            pl.BlockSpec((tm, d), lambda i: (i, 0)),
            pl.BlockSpec((tm, TOP_K), lambda i: (i, 0)),
            pl.BlockSpec((tm, TOP_K), lambda i: (i, 0)),
            pl.BlockSpec((tm, TOP_K), lambda i: (i, 0)),
            pl.BlockSpec((1, ne), lambda i: (0, 0)),
        ],
        out_shape=[
            jax.ShapeDtypeStruct((t, d), BF16),
            jax.ShapeDtypeStruct((t, TOP_K), I32),
            jax.ShapeDtypeStruct((t, TOP_K), F32),
            jax.ShapeDtypeStruct((t, TOP_K), I32),
            jax.ShapeDtypeStruct((1, ne), F32),
        ],
        compiler_params=_cparams(("arbitrary",)),
        name="moe_router",
    )(x, g, rw, rb)


def _moe_kernel(ie_ref, ib_ref, in_ref, x_ref, wg_ref, wu_ref, wd_ref, bg_ref, bu_ref, bd_ref,
                o_ref, act_ref, *, nfa):
    w = pl.program_id(0)
    j = pl.program_id(1)
    nblk = in_ref[w]
    group = 4
    group_rows = group * MOE_BLOCK
    ngroup = nblk // group

    def for_row_groups(fn):
        def body(p, carry):
            fn(pl.multiple_of(p * group_rows, group_rows), group_rows)
            return carry

        lax.fori_loop(0, ngroup, body, 0)
        for rem in range(1, group):
            @pl.when(nblk % group == rem)
            def _():
                fn(pl.multiple_of(ngroup * group_rows, group_rows), rem * MOE_BLOCK)

    @pl.when((nblk > 0) & (j < nfa))
    def _():
        wg = wg_ref[0, 0].astype(BF16)
        wu = wu_ref[0, 0].astype(BF16)
        bg = bg_ref[0, 0]
        bu = bu_ref[0, 0]
        cols = pl.ds(pl.multiple_of(j * MOE_TFA, MOE_TFA), MOE_TFA)

        def gate_up(r0, nrows):
            xr = x_ref[pl.ds(r0, nrows), :]
            g = jnp.dot(xr, wg, preferred_element_type=F32) + bg
            u = jnp.dot(xr, wu, preferred_element_type=F32) + bu
            g = jnp.minimum(g, SWIGLU_LIMIT)
            u = jnp.clip(u, -SWIGLU_LIMIT, SWIGLU_LIMIT)
            act = (u + 1.0) * (g * _sigmoid(g * SWIGLU_ALPHA))
            act_ref[pl.ds(r0, nrows), cols] = act.astype(BF16)

        for_row_groups(gate_up)

    @pl.when((nblk > 0) & (j >= nfa))
    def _():
        wd = wd_ref[0, 0].astype(BF16)
        bd = bd_ref[0, 0]

        def down(r0, nrows):
            y = jnp.dot(act_ref[pl.ds(r0, nrows), :], wd, preferred_element_type=F32) + bd
            o_ref[pl.ds(r0, nrows), :] = y.astype(o_ref.dtype)

        for_row_groups(down)


def moe_experts(xs, item_expert, item_blk0, item_nblk, layer, w_gu, b_gu, w_down, b_down):
    p_rows, d = xs.shape
    depth, ne, _, dff2 = w_gu.shape
    dff = dff2 // 2
    nfa = dff // MOE_TFA
    nfb = d // MOE_TNB
    n_items = item_expert.shape[0]
    item_rows = MOE_ITEM_BLOCKS * MOE_BLOCK

    def ja(w, j, inb):
        return jnp.where(inb[w] > 0, jnp.minimum(j, nfa - 1), nfa - 1)

    def jb(w, j, inb):
        return jnp.where(inb[w] > 0, jnp.maximum(j - nfa, 0), nfb - 1)

    b_gu4 = b_gu.reshape(depth, ne, 1, dff2)
    b_down4 = b_down.reshape(depth, ne, 1, d)
    grid_spec = pltpu.PrefetchScalarGridSpec(
        num_scalar_prefetch=3,
        grid=(n_items, nfa + nfb),
        in_specs=[
            pl.BlockSpec((pl.Element(item_rows), pl.Element(d)),
                         lambda w, j, ie, ib, inb: (ib[w] * MOE_BLOCK, 0)),
            pl.BlockSpec((1, 1, d, MOE_TFA), lambda w, j, ie, ib, inb: (layer, ie[w], 0, ja(w, j, inb))),
            pl.BlockSpec((1, 1, d, MOE_TFA), lambda w, j, ie, ib, inb: (layer, ie[w], 0, nfa + ja(w, j, inb))),
            pl.BlockSpec((1, 1, dff, MOE_TNB), lambda w, j, ie, ib, inb: (layer, ie[w], 0, jb(w, j, inb))),
            pl.BlockSpec((1, 1, 1, MOE_TFA), lambda w, j, ie, ib, inb: (layer, ie[w], 0, ja(w, j, inb))),
            pl.BlockSpec((1, 1, 1, MOE_TFA), lambda w, j, ie, ib, inb: (layer, ie[w], 0, nfa + ja(w, j, inb))),
            pl.BlockSpec((1, 1, 1, MOE_TNB), lambda w, j, ie, ib, inb: (layer, ie[w], 0, jb(w, j, inb))),
        ],
        out_specs=pl.BlockSpec((item_rows, MOE_TNB), lambda w, j, ie, ib, inb: (w, jb(w, j, inb))),
        scratch_shapes=[pltpu.VMEM((item_rows, dff), BF16)],
    )
    return pl.pallas_call(
        functools.partial(_moe_kernel, nfa=nfa),
        grid_spec=grid_spec,
        out_shape=jax.ShapeDtypeStruct((n_items * item_rows, d), BF16),
        compiler_params=_cparams(("arbitrary", "arbitrary")),
        name="moe_experts",
    )(item_expert, item_blk0, item_nblk, xs, w_gu, w_gu, w_down, b_gu4, b_gu4, b_down4)


def _combine_kernel(*refs, final):
    y_refs = refs[:TOP_K]
    g_ref, x_ref = refs[TOP_K], refs[TOP_K + 1]
    o_ref = refs[-1]
    g = g_ref[...]
    acc = x_ref[...]
    for k in range(TOP_K):
        acc = acc + g[:, k:k + 1] * y_refs[k][...].astype(F32)
    if final:
        acc = _rms(acc, refs[TOP_K + 2][...])
    o_ref[...] = acc


def moe_combine(yk, gate4, x, final_g=None, tm=256):
    t, d = x.shape
    final = final_g is not None
    nt = t // tm

    def y_map(k):
        return lambda i: (k * nt + i, 0)

    in_specs = [pl.BlockSpec((tm, d), y_map(k)) for k in range(TOP_K)] + [
        pl.BlockSpec((tm, TOP_K), lambda i: (i, 0)),
        pl.BlockSpec((tm, d), lambda i: (i, 0)),
    ]
    args = [yk] * TOP_K + [gate4, x]
    if final:
        in_specs.append(pl.BlockSpec((1, d), lambda i: (0, 0)))
        args.append(final_g.reshape(1, d))
    return pl.pallas_call(
        functools.partial(_combine_kernel, final=final),
        grid=(t // tm,),
        in_specs=in_specs,
        out_specs=pl.BlockSpec((tm, d), lambda i: (i, 0)),
        out_shape=jax.ShapeDtypeStruct((t, d), F32),
        compiler_params=_cparams(("parallel",)),
        name="moe_combine",
    )(*args)


def _lookup(table, idx):
    onehot = idx[..., None] == jnp.arange(table.shape[0], dtype=I32)
    return jnp.sum(jnp.where(onehot, table, 0), axis=-1)


def moe_layer(x, norm_g, router_w, router_b, layer, w_gu, b_gu, w_down, b_down, final_g=None):
    t, d = x.shape
    ne = router_w.shape[1]
    hn, top_idx, gate4, rank4, cnt = router(x, norm_g.reshape(1, d), router_w, router_b.reshape(1, ne))

    n_assign = t * TOP_K
    n_blocks = -(-n_assign // MOE_BLOCK) + ne
    p_rows = n_blocks * MOE_BLOCK
    item_rows = MOE_ITEM_BLOCKS * MOE_BLOCK
    counts = cnt[0].astype(I32)
    nb = (counts + MOE_BLOCK - 1) // MOE_BLOCK
    blk_start = jnp.cumsum(nb) - nb

    max_items = n_blocks // MOE_ITEM_BLOCKS + ne
    n_it = (nb + MOE_ITEM_BLOCKS - 1) // MOE_ITEM_BLOCKS
    it_cum = jnp.cumsum(n_it)
    it_start = it_cum - n_it
    total_items = it_cum[-1]
    base_e = nb // jnp.maximum(n_it, 1)
    rem_e = nb % jnp.maximum(n_it, 1)
    wids = jnp.arange(max_items, dtype=I32)
    e_of = jnp.minimum(jnp.searchsorted(it_cum, wids, side="right"), ne - 1).astype(I32)
    local = wids - it_start[e_of]
    size = base_e[e_of] + (local < rem_e[e_of]).astype(I32)
    off = local * base_e[e_of] + jnp.minimum(local, rem_e[e_of])
    valid = wids < total_items
    last_e = e_of[jnp.maximum(total_items - 1, 0)]
    item_expert = jnp.where(valid, e_of, last_e).astype(I32)
    item_nblk = jnp.where(valid, size, 0).astype(I32)
    item_blk0 = jnp.where(valid, blk_start[e_of] + off, 0).astype(I32)

    a_e = top_idx.T.reshape(-1)
    a_rank = rank4.T.reshape(-1)
    blk_in_e = a_rank // MOE_BLOCK
    within = a_rank % MOE_BLOCK
    a_base = _lookup(base_e, a_e)
    a_rem = _lookup(rem_e, a_e)
    dest = (_lookup(blk_start, a_e) + blk_in_e) * MOE_BLOCK + within
    big = a_base + 1
    n_big = a_rem * big
    in_big = blk_in_e < n_big
    num = jnp.where(in_big, blk_in_e, blk_in_e - n_big)
    den = jnp.where(in_big, big, jnp.maximum(a_base, 1))
    quo = jnp.floor((num.astype(F32) + 0.5) / den.astype(F32)).astype(I32)
    a_local = jnp.where(in_big, 0, a_rem) + quo
    a_slot = num - quo * den
    out_row = ((_lookup(it_start, a_e) + a_local) * MOE_ITEM_BLOCKS + a_slot) * MOE_BLOCK + within

    n_rows = p_rows + item_rows
    row_token = (jnp.arange(n_rows, dtype=I32) % t).at[dest].set(jnp.tile(jnp.arange(t, dtype=I32), TOP_K))
    xs = hn[row_token]
    ys = moe_experts(xs, item_expert, item_blk0, item_nblk, layer, w_gu, b_gu, w_down, b_down)
    yk = ys[out_row]
    return moe_combine(yk, gate4, x, final_g)


def _indexer_kernel(qi_ref, kit_ref, wi_ref, o_ref, key_ref, wb_ref, cnt_ref, *, tq, topk):
    i = pl.program_id(1)
    nch = i + 1
    o_ref[...] = jnp.full(o_ref.shape, NEG_BIG, o_ref.dtype)
    row = lax.broadcasted_iota(I32, (tq, tq), 0)
    col = lax.broadcasted_iota(I32, (tq, tq), 1)
    wi = wi_ref[...]
    for h in range(IDX_HEADS):
        wb_ref[h] = jnp.broadcast_to(wi[:, h:h + 1], (tq, 128))

    def chunk_slice(c):
        return pl.ds(pl.multiple_of(c * tq, tq), tq)

    def score_chunk(c, carry):
        cs = chunk_slice(c)
        kc = kit_ref[0, :, cs]
        acc = jnp.zeros((tq, tq), F32)
        for h in range(IDX_HEADS):
            s = jnp.dot(qi_ref[:, h * IDX_DIM:(h + 1) * IDX_DIM], kc, preferred_element_type=F32)
            wh = wb_ref[h]
            acc = acc + jnp.concatenate([wh] * (tq // 128), axis=1) * jnp.maximum(s, 0.0)
        bits = pltpu.bitcast(acc, I32)
        key = bits ^ ((bits >> 31) & 0x7FFFFFFF)
        key = jnp.where((c < i) | (col <= row), key, INT_MIN)
        key_ref[:, cs] = key
        return carry

    lax.fori_loop(0, nch, score_chunk, 0)

    pos = i * tq + lax.broadcasted_iota(I32, (tq, 128), 0)
    kk = jnp.minimum(pos + 1, topk).astype(F32)

    @pl.when(nch % 2 == 1)
    def _():
        key_ref[:, chunk_slice(nch)] = jnp.full((tq, tq), INT_MIN, I32)

    cnt_rows = 64
    cnt_cols = 2 * tq

    ones = jnp.ones((128, 128), BF16)

    def body(it, v):
        cand = v | jnp.left_shift(jnp.int32(1), 31 - it)
        thr = cand ^ INT_MIN
        for r0 in range(0, tq, cnt_rows):
            thr_r = thr[r0:r0 + cnt_rows]

            def cnt_step(c, acc):
                kch = key_ref[r0:r0 + cnt_rows, pl.ds(pl.multiple_of(c * cnt_cols, cnt_cols), cnt_cols)]
                for q in range(cnt_cols // 128):
                    acc = acc + jnp.where(kch[:, q * 128:(q + 1) * 128] >= thr_r, 1.0, 0.0)
                return acc

            cnt_ref[r0:r0 + cnt_rows, :] = lax.fori_loop(0, (nch + 1) // 2, cnt_step,
                                                         jnp.zeros((cnt_rows, 128), F32))
        cnt = jnp.dot(cnt_ref[...].astype(BF16), ones, preferred_element_type=F32)
        return jnp.where(cnt >= kk, cand, v)

    v = lax.fori_loop(0, 32, body, jnp.zeros((tq, 128), I32))
    thr = jnp.concatenate([v ^ INT_MIN] * (tq // 128), axis=1)

    def out_chunk(c, carry):
        cs = chunk_slice(c)
        o_ref[:, cs] = jnp.where(key_ref[:, cs] >= thr, 0.0, NEG_BIG).astype(o_ref.dtype)
        return carry

    lax.fori_loop(0, nch, out_chunk, 0)


def indexer_mask(proj, kit, wi, batch, seq, qi_col_block):
    t = proj.shape[0]
    tq = ATT_TILE
    nq = seq // tq
    qi_cols = IDX_HEADS * IDX_DIM
    topk = min(INDEX_TOPK_MAX, seq // 4)
    kern = functools.partial(_indexer_kernel, tq=tq, topk=topk)
    return pl.pallas_call(
        kern,
        grid=(batch, nq),
        in_specs=[
            pl.BlockSpec((tq, qi_cols), lambda b, i: (b * nq + i, qi_col_block)),
            pl.BlockSpec((1, IDX_DIM, seq), lambda b, i: (b, 0, 0)),
            pl.BlockSpec((tq, IDX_HEADS), lambda b, i: (b * nq + i, 0)),
        ],
        out_specs=pl.BlockSpec((tq, seq), lambda b, i: (b * nq + i, 0)),
        out_shape=jax.ShapeDtypeStruct((t, seq), BF16),
        scratch_shapes=[pltpu.VMEM((tq, seq), I32), pltpu.VMEM((IDX_HEADS, tq, 128), F32),
                        pltpu.VMEM((tq, 128), F32)],
        compiler_params=_cparams(("parallel", "parallel")),
        name="dsa_indexer",
    )(proj, kit, wi)


def _attn_kernel(qt_ref, kt_ref, q_ref, k_ref, v_ref, mb_ref, bias_ref, o_ref, acc_ref, m_ref,
                 *, n_heads, group):
    i = qt_ref[pl.program_id(1)]
    j = kt_ref[pl.program_id(1)]
    hd = HEAD_DIM
    tq, tk = mb_ref.shape

    @pl.when(j == 0)
    def _():
        acc_ref[...] = jnp.zeros(acc_ref.shape, F32)
        m_ref[...] = jnp.full(m_ref.shape, NEG_BIG, F32)

    def heads(near):
        mb = mb_ref[...].astype(F32)
        off = i - j
        ones = jnp.ones((tk, 128), BF16)
        v_ext = [jnp.concatenate([v_ref[:, n * hd:(n + 1) * hd], ones], axis=1) for n in range(n_heads // group)]
        for h in range(n_heads):
            n = h // group
            qh = q_ref[:, h * hd:(h + 1) * hd]
            kn = k_ref[:, n * hd:(n + 1) * hd]
            s = lax.dot_general(qh, kn, (((1,), (1,)), ((), ())), preferred_element_type=F32)
            if near:
                s = s + bias_ref[off, h]
            s = s + mb
            m_prev = m_ref[h]
            m_cur = jnp.max(s, axis=1, keepdims=True)
            m_next = jnp.maximum(m_prev, m_cur)
            alpha = jnp.exp2(m_prev - m_next)
            p = jnp.exp2(s - jnp.concatenate([m_next] * (tk // 128), axis=1))
            m_ref[h] = m_next
            pv = jnp.dot(p.astype(BF16), v_ext[n], preferred_element_type=F32)
            acc_ref[h] = acc_ref[h] * jnp.concatenate([alpha, alpha], axis=1) + pv

    @pl.when(i - j < 2)
    def _():
        heads(True)

    @pl.when(i - j >= 2)
    def _():
        heads(False)

    @pl.when(j == i)
    def _():
        for h in range(n_heads):
            a = acc_ref[h]
            o_ref[:, h * hd:(h + 1) * hd] = (a[:, :hd] / a[:, hd:]).astype(o_ref.dtype)


def sparse_attention(proj, mask_bias, bias_tiles, batch, seq, n_heads):
    t = proj.shape[0]
    tq = tk = ATT_TILE
    nq = seq // tq
    q_cols = n_heads * HEAD_DIM
    kv_cols = N_KV_HEADS * HEAD_DIM
    k_blk = q_cols // kv_cols
    kern = functools.partial(_attn_kernel, n_heads=n_heads, group=n_heads // N_KV_HEADS)
    pairs = [(i, j) for i in range(nq) for j in range(i + 1)]
    q_tile = jnp.asarray([p[0] for p in pairs], I32)
    k_tile = jnp.asarray([p[1] for p in pairs], I32)
    grid_spec = pltpu.PrefetchScalarGridSpec(
        num_scalar_prefetch=2,
        grid=(batch, len(pairs)),
        in_specs=[
            pl.BlockSpec((tq, q_cols), lambda b, p, qt, kt: (b * nq + qt[p], 0)),
            pl.BlockSpec((tk, kv_cols), lambda b, p, qt, kt: (b * nq + kt[p], k_blk)),
            pl.BlockSpec((tk, kv_cols), lambda b, p, qt, kt: (b * nq + kt[p], k_blk + 1)),
            pl.BlockSpec((tq, tk), lambda b, p, qt, kt: (b * nq + qt[p], kt[p])),
            pl.BlockSpec((2, n_heads, tq, tk), lambda b, p, qt, kt: (0, 0, 0, 0)),
        ],
        out_specs=pl.BlockSpec((tq, q_cols), lambda b, p, qt, kt: (b * nq + qt[p], 0)),
        scratch_shapes=[
            pltpu.VMEM((n_heads, tq, 2 * HEAD_DIM), F32),
            pltpu.VMEM((n_heads, tq, 128), F32),
        ],
    )
    return pl.pallas_call(
        kern,
        grid_spec=grid_spec,
        out_shape=jax.ShapeDtypeStruct((t, q_cols), BF16),
        compiler_params=_cparams(("parallel", "arbitrary")),
        name="dsa_attention",
    )(q_tile, k_tile, proj, proj, proj, mask_bias, bias_tiles)


def _t5_bucket(n):
    n = jnp.maximum(n, 0)
    max_exact = NUM_BUCKETS // 2
    nf = jnp.maximum(n, 1).astype(F32)
    large = max_exact + (jnp.log(nf / max_exact) / math.log(MAX_DISTANCE / max_exact)
                         * (NUM_BUCKETS - max_exact)).astype(I32)
    large = jnp.minimum(large, NUM_BUCKETS - 1)
    return jnp.where(n < max_exact, n, large)


def _toeplitz(by_delta, n):
    h = by_delta.shape[0]
    u = jnp.concatenate([by_delta[:, :n][:, ::-1], jnp.zeros((h, 2), by_delta.dtype),
                         by_delta[:, n:][:, ::-1]], axis=1)
    rows = jnp.tile(u, (1, n))[:, :n * 2 * n].reshape(h, n, 2 * n)
    return rows[:, :, :n]


def attention_layer(x, norm_g, w_in, w_o, rel_bias, batch, seq):
    t, d = x.shape
    n_heads = w_o.shape[0] // HEAD_DIM
    q_cols = n_heads * HEAD_DIM
    kv_cols = N_KV_HEADS * HEAD_DIM
    qi_cols = IDX_HEADS * IDX_DIM
    main_cols = q_cols + 2 * kv_cols + qi_cols
    assert main_cols % qi_cols == 0 and q_cols % kv_cols == 0
    tail_cols = 128
    w_main = w_in[:, :main_cols].astype(BF16)
    w_tail = jnp.pad(w_in[:, main_cols:], ((0, 0), (0, tail_cols - (IDX_DIM + IDX_HEADS)))).astype(BF16)
    s_main = jnp.concatenate([jnp.full((q_cols,), (HEAD_DIM ** -0.5) * LOG2E, F32),
                              jnp.ones((main_cols - q_cols,), F32)]).reshape(1, main_cols)
    s_tail = jnp.concatenate([jnp.ones((IDX_DIM,), F32),
                              jnp.full((IDX_HEADS,), (IDX_HEADS ** -0.5) * (IDX_DIM ** -0.5), F32),
                              jnp.ones((tail_cols - IDX_DIM - IDX_HEADS,), F32)]).reshape(1, tail_cols)
    g2 = norm_g.reshape(1, d)
    proj = norm_matmul(x, g2, w_main, jnp.zeros((1, main_cols), F32), s_main, BF16)
    tail = norm_matmul(x, g2, w_tail, jnp.zeros((1, tail_cols), F32), s_tail, F32)
    kit = tail[:, :IDX_DIM].astype(BF16).reshape(batch, seq, IDX_DIM).transpose(0, 2, 1)
    wi = tail[:, IDX_DIM:IDX_DIM + IDX_HEADS]
    mask_bias = indexer_mask(proj, kit, wi, batch, seq, (q_cols + 2 * kv_cols) // qi_cols)

    tile = ATT_TILE
    assert tile >= MAX_DISTANCE
    dist = jnp.arange(2 * tile, dtype=I32)
    onehot = (_t5_bucket(dist)[:, None] == jnp.arange(NUM_BUCKETS, dtype=I32)).astype(F32)
    tbl = jnp.dot(onehot, rel_bias - rel_bias[NUM_BUCKETS - 1][None, :], precision=lax.Precision.HIGHEST)
    tbl = jnp.where((dist >= MAX_DISTANCE)[:, None], 0.0, tbl).T
    by_delta0 = jnp.concatenate([jnp.zeros((n_heads, tile - 1), F32), tbl[:, :tile]], axis=1)
    by_delta1 = tbl[:, 1:]
    bias_tiles = jnp.stack([_toeplitz(by_delta0, tile), _toeplitz(by_delta1, tile)], axis=0) * LOG2E

    o = sparse_attention(proj, mask_bias, bias_tiles, batch, seq, n_heads)
    return matmul_residual(o, w_o.astype(BF16), jnp.zeros((1, d), F32), x)


def conv_layer(x, norm_g, w_pw1, b_pw1, w_dw, b_dw, ln_g, ln_b, w_pw2, b_pw2, seq):
    t, d = x.shape
    u = norm_matmul_glu(x, norm_g.reshape(1, d), w_pw1.astype(BF16), b_pw1.reshape(1, 2 * d))
    fold = (CONV_FOLD, d // CONV_FOLD)
    w_dw2 = jnp.pad(w_dw[:, 0, :], ((0, CONV_HALO - CONV_WIDTH), (0, 0))).reshape(CONV_HALO, *fold)
    v = dwconv_ln_swish(u.reshape(t, *fold), w_dw2, b_dw.reshape(1, *fold), ln_g.reshape(1, *fold),
                        ln_b.reshape(1, *fold), seq)
    return matmul_residual(v.reshape(t, d).astype(BF16), w_pw2.astype(BF16), b_pw2.reshape(1, d), x)


def kernel(x, norm_mix, norm_ffn, final_norm, conv_w_pw1, conv_b_pw1, conv_w_dw, conv_b_dw, conv_ln_g,
           conv_ln_b, conv_w_pw2, conv_b_pw2, attn_w_in, attn_w_o, rel_bias, router_w, router_b, moe_w_gu,
           moe_b_gu, moe_w_down, moe_b_down):
    batch, seq, d = x.shape
    depth = norm_mix.shape[0]
    h = x.reshape(batch * seq, d)
    for i in range(depth):
        jdx = i // 2
        if i % 2 == 0:
            h = conv_layer(h, norm_mix[i], conv_w_pw1[jdx], conv_b_pw1[jdx], conv_w_dw[jdx], conv_b_dw[jdx],
                           conv_ln_g[jdx], conv_ln_b[jdx], conv_w_pw2[jdx], conv_b_pw2[jdx], seq)
        else:
            h = attention_layer(h, norm_mix[i], attn_w_in[jdx], attn_w_o[jdx], rel_bias, batch, seq)
        h = moe_layer(h, norm_ffn[i], router_w[i], router_b[i], i, moe_w_gu, moe_b_gu, moe_w_down, moe_b_down,
                      final_g=final_norm if i == depth - 1 else None)
    return h.reshape(batch, seq, d)
```

```python
import functools
import math

import numpy as np
import jax
import jax.numpy as jnp
from jax import lax
from jax.experimental import pallas as pl
from jax.experimental.pallas import tpu as pltpu

F32 = jnp.float32
BF16 = jnp.bfloat16
I32 = jnp.int32

NORM_EPS = 1e-5
CONV_WIDTH = 31
HEAD_DIM = 128
N_KV_HEADS = 4
IDX_HEADS = 16
IDX_DIM = 64
INDEX_TOPK_MAX = 256
NUM_BUCKETS = 32
MAX_DISTANCE = 128
N_EXPERTS = 32
TOP_K = 4
SWIGLU_LIMIT = 7.0
SWIGLU_ALPHA = 1.702
MOE_BLOCK = 256
MOE_ITEM_BLOCKS = 10
MOE_TFA = 256
MOE_TNB = 256
ATT_TILE = 256
CONV_HALO = 32
CONV_FOLD = 8
NEG_BIG = -1e30
INT_MIN = -2147483648
LOG2E = math.log2(math.e)
VMEM_LIMIT = 56 * 1024 * 1024


def _cparams(sem):
    return pltpu.CompilerParams(dimension_semantics=sem, vmem_limit_bytes=VMEM_LIMIT)


def _rms(x, g):
    ms = jnp.mean(x * x, axis=-1, keepdims=True)
    return (x * lax.rsqrt(ms + NORM_EPS)) * g


def _sigmoid(x):
    return 1.0 / (1.0 + jnp.exp(-x))


def _norm_mm_kernel(x_ref, g_ref, w_ref, b_ref, s_ref, o_ref, hn_ref):
    @pl.when(pl.program_id(1) == 0)
    def _():
        hn_ref[...] = _rms(x_ref[...], g_ref[...]).astype(BF16)

    acc = jnp.dot(hn_ref[...], w_ref[...], preferred_element_type=F32)
    o_ref[...] = ((acc + b_ref[...]) * s_ref[...]).astype(o_ref.dtype)


def norm_matmul(x, g, w, b, s, out_dtype, tm=1024, tn=512):
    t, d = x.shape
    n = w.shape[1]
    tn = min(tn, n)
    return pl.pallas_call(
        _norm_mm_kernel,
        grid=(t // tm, n // tn),
        in_specs=[
            pl.BlockSpec((tm, d), lambda i, j: (i, 0)),
            pl.BlockSpec((1, d), lambda i, j: (0, 0)),
            pl.BlockSpec((d, tn), lambda i, j: (0, j)),
            pl.BlockSpec((1, tn), lambda i, j: (0, j)),
            pl.BlockSpec((1, tn), lambda i, j: (0, j)),
        ],
        out_specs=pl.BlockSpec((tm, tn), lambda i, j: (i, j)),
        out_shape=jax.ShapeDtypeStruct((t, n), out_dtype),
        scratch_shapes=[pltpu.VMEM((tm, d), BF16)],
        compiler_params=_cparams(("parallel", "arbitrary")),
        name="norm_matmul",
    )(x, g, w, b, s)


def _norm_mm_glu_kernel(x_ref, g_ref, wa_ref, wg_ref, ba_ref, bg_ref, o_ref, hn_ref):
    @pl.when(pl.program_id(1) == 0)
    def _():
        hn_ref[...] = _rms(x_ref[...], g_ref[...]).astype(BF16)

    hn = hn_ref[...]
    a = jnp.dot(hn, wa_ref[...], preferred_element_type=F32) + ba_ref[...]
    gt = jnp.dot(hn, wg_ref[...], preferred_element_type=F32) + bg_ref[...]
    o_ref[...] = (a * _sigmoid(gt)).astype(o_ref.dtype)


def norm_matmul_glu(x, g, w, b, tm=1024, tn=512):
    t, d = x.shape
    n = w.shape[1] // 2
    nj = n // tn
    return pl.pallas_call(
        _norm_mm_glu_kernel,
        grid=(t // tm, nj),
        in_specs=[
            pl.BlockSpec((tm, d), lambda i, j: (i, 0)),
            pl.BlockSpec((1, d), lambda i, j: (0, 0)),
            pl.BlockSpec((d, tn), lambda i, j: (0, j)),
            pl.BlockSpec((d, tn), lambda i, j: (0, j + nj)),
            pl.BlockSpec((1, tn), lambda i, j: (0, j)),
            pl.BlockSpec((1, tn), lambda i, j: (0, j + nj)),
        ],
        out_specs=pl.BlockSpec((tm, tn), lambda i, j: (i, j)),
        out_shape=jax.ShapeDtypeStruct((t, n), F32),
        scratch_shapes=[pltpu.VMEM((tm, d), BF16)],
        compiler_params=_cparams(("parallel", "arbitrary")),
        name="norm_pw1_glu",
    )(x, g, w, w, b, b)


def _mm_res_kernel(a_ref, w_ref, b_ref, r_ref, o_ref):
    acc = jnp.dot(a_ref[...], w_ref[...], preferred_element_type=F32)
    o_ref[...] = r_ref[...] + (acc + b_ref[...])


def matmul_residual(a, w, b, res, tm=1024, tn=512):
    t, k = a.shape
    n = w.shape[1]
    return pl.pallas_call(
        _mm_res_kernel,
        grid=(t // tm, n // tn),
        in_specs=[
            pl.BlockSpec((tm, k), lambda i, j: (i, 0)),
            pl.BlockSpec((k, tn), lambda i, j: (0, j)),
            pl.BlockSpec((1, tn), lambda i, j: (0, j)),
            pl.BlockSpec((tm, tn), lambda i, j: (i, j)),
        ],
        out_specs=pl.BlockSpec((tm, tn), lambda i, j: (i, j)),
        out_shape=jax.ShapeDtypeStruct((t, n), F32),
        compiler_params=_cparams(("parallel", "parallel")),
        name="matmul_residual",
    )(a, w, b, res)


def _dwconv_kernel(prev_ref, cur_ref, w_ref, bdw_ref, lg_ref, lb_ref, o_ref, buf_ref,
                   *, ts, tiles_per_seq, row_chunk, norm_chunk):
    i = pl.program_id(0)
    first = (i % tiles_per_seq) == 0
    buf_ref[0:CONV_HALO] = jnp.where(first, 0.0, prev_ref[...])
    buf_ref[CONV_HALO:] = cur_ref[...]
    _, cs, cl = cur_ref.shape
    inv_n = 1.0 / (cs * cl)
    shift = CONV_HALO - (CONV_WIDTH - 1)

    def row_body(rc, carry):
        r0 = pl.multiple_of(rc * row_chunk, row_chunk)
        parts = [jnp.broadcast_to(bdw_ref[...], (row_chunk, cs, cl)), jnp.zeros((row_chunk, cs, cl), F32)]
        for k in range(CONV_WIDTH):
            parts[k % 2] = parts[k % 2] + w_ref[k] * buf_ref[pl.ds(r0 + k + shift, row_chunk)]
        o_ref[pl.ds(r0, row_chunk)] = parts[0] + parts[1]
        return carry

    lax.fori_loop(0, ts // row_chunk, row_body, 0)

    def norm_body(rc, carry):
        rows = pl.ds(pl.multiple_of(rc * norm_chunk, norm_chunk), norm_chunk)
        y = o_ref[rows]
        mu = jnp.sum(y, axis=(1, 2), keepdims=True) * inv_n
        yc = y - mu
        var = jnp.sum(yc * yc, axis=(1, 2), keepdims=True) * inv_n
        z = (yc * lax.rsqrt(var + NORM_EPS)) * lg_ref[...] + lb_ref[...]
        o_ref[rows] = z * _sigmoid(z)
        return carry

    lax.fori_loop(0, ts // norm_chunk, norm_body, 0)


def dwconv_ln_swish(u, w_dw, b_dw, ln_g, ln_b, seq, ts=256):
    t, cs, cl = u.shape
    hb = ts // CONV_HALO
    kern = functools.partial(_dwconv_kernel, ts=ts, tiles_per_seq=seq // ts, row_chunk=8, norm_chunk=32)
    vec = pl.BlockSpec((1, cs, cl), lambda i: (0, 0, 0))
    return pl.pallas_call(
        kern,
        grid=(t // ts,),
        in_specs=[
            pl.BlockSpec((CONV_HALO, cs, cl), lambda i: (jnp.maximum(i * hb - 1, 0), 0, 0)),
            pl.BlockSpec((ts, cs, cl), lambda i: (i, 0, 0)),
            pl.BlockSpec((CONV_HALO, cs, cl), lambda i: (0, 0, 0)),
            vec, vec, vec,
        ],
        out_specs=pl.BlockSpec((ts, cs, cl), lambda i: (i, 0, 0)),
        out_shape=jax.ShapeDtypeStruct((t, cs, cl), F32),
        scratch_shapes=[pltpu.VMEM((ts + CONV_HALO, cs, cl), F32)],
        compiler_params=_cparams(("parallel",)),
        name="dwconv_ln_swish",
    )(u, u, w_dw, b_dw, ln_g, ln_b)


def _router_kernel(x_ref, g_ref, rw_ref, rb_ref, hn_ref, idx_ref, gate_ref, rank_ref, cnt_ref):
    @pl.when(pl.program_id(0) == 0)
    def _():
        cnt_ref[...] = jnp.zeros(cnt_ref.shape, F32)

    hn = _rms(x_ref[...], g_ref[...])
    hn_ref[...] = hn.astype(BF16)
    logits = jnp.dot(hn, rw_ref[...], preferred_element_type=F32,
                     precision=lax.Precision.HIGHEST) + rb_ref[...]
    tm, ne = logits.shape
    lane = lax.broadcasted_iota(I32, (tm, ne), 1).astype(F32)
    lane4 = lax.broadcasted_iota(I32, (tm, TOP_K), 1)
    work = logits
    vals, hits = [], []
    idx_out = jnp.zeros((tm, TOP_K), F32)
    for k in range(TOP_K):
        m = jnp.max(work, axis=-1, keepdims=True)
        idx = jnp.min(jnp.where(work == m, lane, float(ne)), axis=-1, keepdims=True)
        hit = lane == idx
        vals.append(m)
        hits.append(hit)
        idx_out = jnp.where(lane4 == k, idx, idx_out)
        work = jnp.where(hit, -jnp.inf, work)
    es = [jnp.exp(v - vals[0]) for v in vals]
    denom = es[0] + es[1] + es[2] + es[3]
    sel = jnp.zeros((tm, ne), F32)
    gate_out = jnp.zeros((tm, TOP_K), F32)
    for k in range(TOP_K):
        sel = jnp.where(hits[k], 1.0, sel)
        gate_out = jnp.where(lane4 == k, es[k] / denom, gate_out)
    r_i = lax.broadcasted_iota(I32, (tm, tm), 0)
    c_i = lax.broadcasted_iota(I32, (tm, tm), 1)
    tri = jnp.where(c_i < r_i, 1.0, 0.0).astype(BF16)
    rank_full = jnp.dot(tri, sel.astype(BF16), preferred_element_type=F32) + cnt_ref[...]
    rank_out = jnp.zeros((tm, TOP_K), F32)
    for k in range(TOP_K):
        rk = jnp.sum(jnp.where(hits[k], rank_full, 0.0), axis=-1, keepdims=True)
        rank_out = jnp.where(lane4 == k, rk, rank_out)
    cnt_ref[...] = cnt_ref[...] + jnp.sum(sel, axis=0, keepdims=True)
    idx_ref[...] = idx_out.astype(I32)
    gate_ref[...] = gate_out
    rank_ref[...] = rank_out.astype(I32)


def router(x, g, rw, rb, tm=512):
    t, d = x.shape
    ne = rw.shape[1]
    return pl.pallas_call(
        _router_kernel,
        grid=(t // tm,),
        in_specs=[
            pl.BlockSpec((tm, d), lambda i: (i, 0)),
            pl.BlockSpec((1, d), lambda i: (0, 0)),
            pl.BlockSpec((d, ne), lambda i: (0, 0)),
            pl.BlockSpec((1, ne), lambda i: (0, 0)),
        ],
        out_specs=[
            pl.BlockSpec((tm, d), lambda i: (i, 0)),
            pl.BlockSpec((tm, TOP_K), lambda i: (i, 0)),
            pl.BlockSpec((tm, TOP_K), lambda i: (i, 0)),
            pl.BlockSpec((tm, TOP_K), lambda i: (i, 0)),
            pl.BlockSpec((1, ne), lambda i: (0, 0)),
        ],
        out_shape=[
            jax.ShapeDtypeStruct((t, d), BF16),
            jax.ShapeDtypeStruct((t, TOP_K), I32),
            jax.ShapeDtypeStruct((t, TOP_K), F32),
            jax.ShapeDtypeStruct((t, TOP_K), I32),
            jax.ShapeDtypeStruct((1, ne), F32),
        ],
        compiler_params=_cparams(("arbitrary",)),
        name="moe_router",
    )(x, g, rw, rb)


def _moe_kernel(ie_ref, ib_ref, in_ref, x_ref, wg_ref, wu_ref, wd_ref, bg_ref, bu_ref, bd_ref,
                o_ref, act_ref, *, nfa):
    w = pl.program_id(0)
    j = pl.program_id(1)
    nblk = in_ref[w]
    group = 4
    group_rows = group * MOE_BLOCK
    ngroup = nblk // group

    def for_row_groups(fn):
        def body(p, carry):
            fn(pl.multiple_of(p * group_rows, group_rows), group_rows)
            return carry

        lax.fori_loop(0, ngroup, body, 0)
        for rem in range(1, group):
            @pl.when(nblk % group == rem)
            def _():
                fn(pl.multiple_of(ngroup * group_rows, group_rows), rem * MOE_BLOCK)

    @pl.when((nblk > 0) & (j < nfa))
    def _():
        wg = wg_ref[0, 0].astype(BF16)
        wu = wu_ref[0, 0].astype(BF16)
        bg = bg_ref[0, 0]
        bu = bu_ref[0, 0]
        cols = pl.ds(pl.multiple_of(j * MOE_TFA, MOE_TFA), MOE_TFA)

        def gate_up(r0, nrows):
            xr = x_ref[pl.ds(r0, nrows), :]
            g = jnp.dot(xr, wg, preferred_element_type=F32) + bg
            u = jnp.dot(xr, wu, preferred_element_type=F32) + bu
            g = jnp.minimum(g, SWIGLU_LIMIT)
            u = jnp.clip(u, -SWIGLU_LIMIT, SWIGLU_LIMIT)
            act = (u + 1.0) * (g * _sigmoid(g * SWIGLU_ALPHA))
            act_ref[pl.ds(r0, nrows), cols] = act.astype(BF16)

        for_row_groups(gate_up)

    @pl.when((nblk > 0) & (j >= nfa))
    def _():
        wd = wd_ref[0, 0].astype(BF16)
        bd = bd_ref[0, 0]

        def down(r0, nrows):
            y = jnp.dot(act_ref[pl.ds(r0, nrows), :], wd, preferred_element_type=F32) + bd
            o_ref[pl.ds(r0, nrows), :] = y.astype(o_ref.dtype)

        for_row_groups(down)


def moe_experts(xs, item_expert, item_blk0, item_nblk, n_used, layer, w_gu, b_gu, w_down, b_down):
    p_rows, d = xs.shape
    depth, ne, _, dff2 = w_gu.shape
    dff = dff2 // 2
    nfa = dff // MOE_TFA
    nfb = d // MOE_TNB
    n_items = item_expert.shape[0]
    item_rows = MOE_ITEM_BLOCKS * MOE_BLOCK

    def ja(w, j, inb):
        return jnp.where(inb[w] > 0, jnp.minimum(j, nfa - 1), nfa - 1)

    def jb(w, j, inb):
        return jnp.where(inb[w] > 0, jnp.maximum(j - nfa, 0), nfb - 1)

    b_gu4 = b_gu.reshape(depth, ne, 1, dff2)
    b_down4 = b_down.reshape(depth, ne, 1, d)
    grid_spec = pltpu.PrefetchScalarGridSpec(
        num_scalar_prefetch=3,
        grid=(n_used, nfa + nfb),
        in_specs=[
            pl.BlockSpec((pl.Element(item_rows), pl.Element(d)),
                         lambda w, j, ie, ib, inb: (ib[w] * MOE_BLOCK, 0)),
            pl.BlockSpec((1, 1, d, MOE_TFA), lambda w, j, ie, ib, inb: (layer, ie[w], 0, ja(w, j, inb))),
            pl.BlockSpec((1, 1, d, MOE_TFA), lambda w, j, ie, ib, inb: (layer, ie[w], 0, nfa + ja(w, j, inb))),
            pl.BlockSpec((1, 1, dff, MOE_TNB), lambda w, j, ie, ib, inb: (layer, ie[w], 0, jb(w, j, inb))),
            pl.BlockSpec((1, 1, 1, MOE_TFA), lambda w, j, ie, ib, inb: (layer, ie[w], 0, ja(w, j, inb))),
            pl.BlockSpec((1, 1, 1, MOE_TFA), lambda w, j, ie, ib, inb: (layer, ie[w], 0, nfa + ja(w, j, inb))),
            pl.BlockSpec((1, 1, 1, MOE_TNB), lambda w, j, ie, ib, inb: (layer, ie[w], 0, jb(w, j, inb))),
        ],
        out_specs=pl.BlockSpec((item_rows, MOE_TNB), lambda w, j, ie, ib, inb: (w, jb(w, j, inb))),
        scratch_shapes=[pltpu.VMEM((item_rows, dff), BF16)],
    )
    return pl.pallas_call(
        functools.partial(_moe_kernel, nfa=nfa),
        grid_spec=grid_spec,
        out_shape=jax.ShapeDtypeStruct((n_items * item_rows, d), BF16),
        compiler_params=_cparams(("arbitrary", "arbitrary")),
        name="moe_experts",
    )(item_expert, item_blk0, item_nblk, xs, w_gu, w_gu, w_down, b_gu4, b_gu4, b_down4)


def _combine_kernel(*refs, final):
    y_refs = refs[:TOP_K]
    g_ref, x_ref = refs[TOP_K], refs[TOP_K + 1]
    o_ref = refs[-1]
    g = g_ref[...]
    acc = x_ref[...]
    for k in range(TOP_K):
        acc = acc + g[:, k:k + 1] * y_refs[k][...].astype(F32)
    if final:
        acc = _rms(acc, refs[TOP_K + 2][...])
    o_ref[...] = acc


def moe_combine(yk, gate4, x, final_g=None, tm=256):
    t, d = x.shape
    final = final_g is not None
    nt = t // tm

    def y_map(k):
        return lambda i: (k * nt + i, 0)

    in_specs = [pl.BlockSpec((tm, d), y_map(k)) for k in range(TOP_K)] + [
        pl.BlockSpec((tm, TOP_K), lambda i: (i, 0)),
        pl.BlockSpec((tm, d), lambda i: (i, 0)),
    ]
    args = [yk] * TOP_K + [gate4, x]
    if final:
        in_specs.append(pl.BlockSpec((1, d), lambda i: (0, 0)))
        args.append(final_g.reshape(1, d))
    return pl.pallas_call(
        functools.partial(_combine_kernel, final=final),
        grid=(t // tm,),
        in_specs=in_specs,
        out_specs=pl.BlockSpec((tm, d), lambda i: (i, 0)),
        out_shape=jax.ShapeDtypeStruct((t, d), F32),
        compiler_params=_cparams(("parallel",)),
        name="moe_combine",
    )(*args)


def _lookup(table, idx):
    onehot = idx[..., None] == jnp.arange(table.shape[0], dtype=I32)
    return jnp.sum(jnp.where(onehot, table, 0), axis=-1)


def moe_layer(x, norm_g, router_w, router_b, layer, w_gu, b_gu, w_down, b_down, final_g=None):
    t, d = x.shape
    ne = router_w.shape[1]
    hn, top_idx, gate4, rank4, cnt = router(x, norm_g.reshape(1, d), router_w, router_b.reshape(1, ne))

    n_assign = t * TOP_K
    n_blocks = -(-n_assign // MOE_BLOCK) + ne
    p_rows = n_blocks * MOE_BLOCK
    item_rows = MOE_ITEM_BLOCKS * MOE_BLOCK
    counts = cnt[0].astype(I32)
    nb = (counts + MOE_BLOCK - 1) // MOE_BLOCK
    blk_start = jnp.cumsum(nb) - nb

    max_items = n_blocks // MOE_ITEM_BLOCKS + ne
    n_it = (nb + MOE_ITEM_BLOCKS - 1) // MOE_ITEM_BLOCKS
    it_cum = jnp.cumsum(n_it)
    it_start = it_cum - n_it
    total_items = it_cum[-1]
    base_e = nb // jnp.maximum(n_it, 1)
    rem_e = nb % jnp.maximum(n_it, 1)
    wids = jnp.arange(max_items, dtype=I32)
    e_of = jnp.minimum(jnp.searchsorted(it_cum, wids, side="right"), ne - 1).astype(I32)
    local = wids - it_start[e_of]
    size = base_e[e_of] + (local < rem_e[e_of]).astype(I32)
    off = local * base_e[e_of] + jnp.minimum(local, rem_e[e_of])
    valid = wids < total_items
    last_e = e_of[jnp.maximum(total_items - 1, 0)]
    item_expert = jnp.where(valid, e_of, last_e).astype(I32)
    item_nblk = jnp.where(valid, size, 0).astype(I32)
    item_blk0 = jnp.where(valid, blk_start[e_of] + off, 0).astype(I32)

    a_e = top_idx.T.reshape(-1)
    a_rank = rank4.T.reshape(-1)
    blk_in_e = a_rank // MOE_BLOCK
    within = a_rank % MOE_BLOCK
    a_base = _lookup(base_e, a_e)
    a_rem = _lookup(rem_e, a_e)
    dest = (_lookup(blk_start, a_e) + blk_in_e) * MOE_BLOCK + within
    big = a_base + 1
    n_big = a_rem * big
    in_big = blk_in_e < n_big
    num = jnp.where(in_big, blk_in_e, blk_in_e - n_big)
    den = jnp.where(in_big, big, jnp.maximum(a_base, 1))
    quo = jnp.floor((num.astype(F32) + 0.5) / den.astype(F32)).astype(I32)
    a_local = jnp.where(in_big, 0, a_rem) + quo
    a_slot = num - quo * den
    out_row = ((_lookup(it_start, a_e) + a_local) * MOE_ITEM_BLOCKS + a_slot) * MOE_BLOCK + within

    n_rows = p_rows + item_rows
    row_token = (jnp.arange(n_rows, dtype=I32) % t).at[dest].set(jnp.tile(jnp.arange(t, dtype=I32), TOP_K))
    xs = hn[row_token]
    ys = moe_experts(xs, item_expert, item_blk0, item_nblk, total_items.astype(I32), layer, w_gu, b_gu,
                     w_down, b_down)
    yk = ys[out_row]
    return moe_combine(yk, gate4, x, final_g)


def _indexer_kernel(qi_ref, kit_ref, wi_ref, o_ref, key_ref, wb_ref, cnt_ref, *, tq, topk):
    i = pl.program_id(1)
    nch = i + 1
    o_ref[...] = jnp.full(o_ref.shape, NEG_BIG, o_ref.dtype)
    row = lax.broadcasted_iota(I32, (tq, tq), 0)
    col = lax.broadcasted_iota(I32, (tq, tq), 1)
    wi = wi_ref[...]
    for h in range(IDX_HEADS):
        wb_ref[h] = jnp.broadcast_to(wi[:, h:h + 1], (tq, 128))

    def chunk_slice(c):
        return pl.ds(pl.multiple_of(c * tq, tq), tq)

    def score_chunk(c, carry):
        cs = chunk_slice(c)
        kc = kit_ref[0, :, cs]
        acc = jnp.zeros((tq, tq), F32)
        for h in range(IDX_HEADS):
            s = jnp.dot(qi_ref[:, h * IDX_DIM:(h + 1) * IDX_DIM], kc, preferred_element_type=F32)
            wh = wb_ref[h]
            acc = acc + jnp.concatenate([wh] * (tq // 128), axis=1) * jnp.maximum(s, 0.0)
        bits = pltpu.bitcast(acc, I32)
        key = bits ^ ((bits >> 31) & 0x7FFFFFFF)
        key = jnp.where((c < i) | (col <= row), key, INT_MIN)
        key_ref[:, cs] = key
        return carry

    lax.fori_loop(0, nch, score_chunk, 0)

    pos = i * tq + lax.broadcasted_iota(I32, (tq, 128), 0)
    kk = jnp.minimum(pos + 1, topk).astype(F32)

    @pl.when(nch % 2 == 1)
    def _():
        key_ref[:, chunk_slice(nch)] = jnp.full((tq, tq), INT_MIN, I32)

    cnt_rows = 64
    cnt_cols = 2 * tq

    ones = jnp.ones((128, 128), BF16)

    def body(it, v):
        cand = v | jnp.left_shift(jnp.int32(1), 31 - it)
        thr = cand ^ INT_MIN
        for r0 in range(0, tq, cnt_rows):
            thr_r = thr[r0:r0 + cnt_rows]

            def cnt_step(c, acc):
                kch = key_ref[r0:r0 + cnt_rows, pl.ds(pl.multiple_of(c * cnt_cols, cnt_cols), cnt_cols)]
                for q in range(cnt_cols // 128):
                    acc = acc + jnp.where(kch[:, q * 128:(q + 1) * 128] >= thr_r, 1.0, 0.0)
                return acc

            cnt_ref[r0:r0 + cnt_rows, :] = lax.fori_loop(0, (nch + 1) // 2, cnt_step,
                                                         jnp.zeros((cnt_rows, 128), F32))
        cnt = jnp.dot(cnt_ref[...].astype(BF16), ones, preferred_element_type=F32)
        return jnp.where(cnt >= kk, cand, v)

    v = lax.fori_loop(0, 32, body, jnp.zeros((tq, 128), I32))
    thr = jnp.concatenate([v ^ INT_MIN] * (tq // 128), axis=1)

    def out_chunk(c, carry):
        cs = chunk_slice(c)
        o_ref[:, cs] = jnp.where(key_ref[:, cs] >= thr, 0.0, NEG_BIG).astype(o_ref.dtype)
        return carry

    lax.fori_loop(0, nch, out_chunk, 0)


def indexer_mask(proj, kit, wi, batch, seq, qi_col_block):
    t = proj.shape[0]
    tq = ATT_TILE
    nq = seq // tq
    qi_cols = IDX_HEADS * IDX_DIM
    topk = min(INDEX_TOPK_MAX, seq // 4)
    kern = functools.partial(_indexer_kernel, tq=tq, topk=topk)
    return pl.pallas_call(
        kern,
        grid=(batch, nq),
        in_specs=[
            pl.BlockSpec((tq, qi_cols), lambda b, i: (b * nq + i, qi_col_block)),
            pl.BlockSpec((1, IDX_DIM, seq), lambda b, i: (b, 0, 0)),
            pl.BlockSpec((tq, IDX_HEADS), lambda b, i: (b * nq + i, 0)),
        ],
        out_specs=pl.BlockSpec((tq, seq), lambda b, i: (b * nq + i, 0)),
        out_shape=jax.ShapeDtypeStruct((t, seq), BF16),
        scratch_shapes=[pltpu.VMEM((tq, seq), I32), pltpu.VMEM((IDX_HEADS, tq, 128), F32),
                        pltpu.VMEM((tq, 128), F32)],
        compiler_params=_cparams(("parallel", "parallel")),
        name="dsa_indexer",
    )(proj, kit, wi)


def _attn_kernel(qt_ref, kt_ref, q_ref, k_ref, v_ref, mb_ref, bias_ref, o_ref, acc_ref, m_ref,
                 *, n_heads, group):
    i = qt_ref[pl.program_id(1)]
    j = kt_ref[pl.program_id(1)]
    hd = HEAD_DIM
    tq, tk = mb_ref.shape

    @pl.when(j == 0)
    def _():
        acc_ref[...] = jnp.zeros(acc_ref.shape, F32)
        m_ref[...] = jnp.full(m_ref.shape, NEG_BIG, F32)

    def heads(near):
        mb = mb_ref[...].astype(F32)
        off = i - j
        ones = jnp.ones((tk, 128), BF16)
        v_ext = [jnp.concatenate([v_ref[:, n * hd:(n + 1) * hd], ones], axis=1) for n in range(n_heads // group)]
        for h in range(n_heads):
            n = h // group
            qh = q_ref[:, h * hd:(h + 1) * hd]
            kn = k_ref[:, n * hd:(n + 1) * hd]
            s = lax.dot_general(qh, kn, (((1,), (1,)), ((), ())), preferred_element_type=F32)
            if near:
                s = s + bias_ref[off, h]
            s = s + mb
            m_prev = m_ref[h]
            m_cur = jnp.max(s, axis=1, keepdims=True)
            m_next = jnp.maximum(m_prev, m_cur)
            alpha = jnp.exp2(m_prev - m_next)
            p = jnp.exp2(s - jnp.concatenate([m_next] * (tk // 128), axis=1))
            m_ref[h] = m_next
            pv = jnp.dot(p.astype(BF16), v_ext[n], preferred_element_type=F32)
            acc_ref[h] = acc_ref[h] * jnp.concatenate([alpha, alpha], axis=1) + pv

    @pl.when(i - j < 2)
    def _():
        heads(True)

    @pl.when(i - j >= 2)
    def _():
        heads(False)

    @pl.when(j == i)
    def _():
        for h in range(n_heads):
            a = acc_ref[h]
            o_ref[:, h * hd:(h + 1) * hd] = (a[:, :hd] / a[:, hd:]).astype(o_ref.dtype)


def sparse_attention(proj, mask_bias, bias_tiles, batch, seq, n_heads):
    t = proj.shape[0]
    tq = tk = ATT_TILE
    nq = seq // tq
    q_cols = n_heads * HEAD_DIM
    kv_cols = N_KV_HEADS * HEAD_DIM
    k_blk = q_cols // kv_cols
    kern = functools.partial(_attn_kernel, n_heads=n_heads, group=n_heads // N_KV_HEADS)
    pairs = [(i, j) for i in range(nq) for j in range(i + 1)]
    q_tile = jnp.asarray([p[0] for p in pairs], I32)
    k_tile = jnp.asarray([p[1] for p in pairs], I32)
    grid_spec = pltpu.PrefetchScalarGridSpec(
        num_scalar_prefetch=2,
        grid=(batch, len(pairs)),
        in_specs=[
            pl.BlockSpec((tq, q_cols), lambda b, p, qt, kt: (b * nq + qt[p], 0)),
            pl.BlockSpec((tk, kv_cols), lambda b, p, qt, kt: (b * nq + kt[p], k_blk)),
            pl.BlockSpec((tk, kv_cols), lambda b, p, qt, kt: (b * nq + kt[p], k_blk + 1)),
            pl.BlockSpec((tq, tk), lambda b, p, qt, kt: (b * nq + qt[p], kt[p])),
            pl.BlockSpec((2, n_heads, tq, tk), lambda b, p, qt, kt: (0, 0, 0, 0)),
        ],
        out_specs=pl.BlockSpec((tq, q_cols), lambda b, p, qt, kt: (b * nq + qt[p], 0)),
        scratch_shapes=[
            pltpu.VMEM((n_heads, tq, 2 * HEAD_DIM), F32),
            pltpu.VMEM((n_heads, tq, 128), F32),
        ],
    )
    return pl.pallas_call(
        kern,
        grid_spec=grid_spec,
        out_shape=jax.ShapeDtypeStruct((t, q_cols), BF16),
        compiler_params=_cparams(("parallel", "arbitrary")),
        name="dsa_attention",
    )(q_tile, k_tile, proj, proj, proj, mask_bias, bias_tiles)


def _t5_bucket(n):
    n = jnp.maximum(n, 0)
    max_exact = NUM_BUCKETS // 2
    nf = jnp.maximum(n, 1).astype(F32)
    large = max_exact + (jnp.log(nf / max_exact) / math.log(MAX_DISTANCE / max_exact)
                         * (NUM_BUCKETS - max_exact)).astype(I32)
    large = jnp.minimum(large, NUM_BUCKETS - 1)
    return jnp.where(n < max_exact, n, large)


def _toeplitz(by_delta, n):
    h = by_delta.shape[0]
    u = jnp.concatenate([by_delta[:, :n][:, ::-1], jnp.zeros((h, 2), by_delta.dtype),
                         by_delta[:, n:][:, ::-1]], axis=1)
    rows = jnp.tile(u, (1, n))[:, :n * 2 * n].reshape(h, n, 2 * n)
    return rows[:, :, :n]


def attention_layer(x, norm_g, w_in, w_o, rel_bias, batch, seq):
    t, d = x.shape
    n_heads = w_o.shape[0] // HEAD_DIM
    q_cols = n_heads * HEAD_DIM
    kv_cols = N_KV_HEADS * HEAD_DIM
    qi_cols = IDX_HEADS * IDX_DIM
    main_cols = q_cols + 2 * kv_cols + qi_cols
    assert main_cols % qi_cols == 0 and q_cols % kv_cols == 0
    tail_cols = 128
    w_main = w_in[:, :main_cols].astype(BF16)
    w_tail = jnp.pad(w_in[:, main_cols:], ((0, 0), (0, tail_cols - (IDX_DIM + IDX_HEADS)))).astype(BF16)
    s_main = jnp.concatenate([jnp.full((q_cols,), (HEAD_DIM ** -0.5) * LOG2E, F32),
                              jnp.ones((main_cols - q_cols,), F32)]).reshape(1, main_cols)
    s_tail = jnp.concatenate([jnp.ones((IDX_DIM,), F32),
                              jnp.full((IDX_HEADS,), (IDX_HEADS ** -0.5) * (IDX_DIM ** -0.5), F32),
                              jnp.ones((tail_cols - IDX_DIM - IDX_HEADS,), F32)]).reshape(1, tail_cols)
    g2 = norm_g.reshape(1, d)
    proj = norm_matmul(x, g2, w_main, jnp.zeros((1, main_cols), F32), s_main, BF16)
    tail = norm_matmul(x, g2, w_tail, jnp.zeros((1, tail_cols), F32), s_tail, F32)
    kit = tail[:, :IDX_DIM].astype(BF16).reshape(batch, seq, IDX_DIM).transpose(0, 2, 1)
    wi = tail[:, IDX_DIM:IDX_DIM + IDX_HEADS]
    mask_bias = indexer_mask(proj, kit, wi, batch, seq, (q_cols + 2 * kv_cols) // qi_cols)

    tile = ATT_TILE
    assert tile >= MAX_DISTANCE
    dist = jnp.arange(2 * tile, dtype=I32)
    onehot = (_t5_bucket(dist)[:, None] == jnp.arange(NUM_BUCKETS, dtype=I32)).astype(F32)
    tbl = jnp.dot(onehot, rel_bias - rel_bias[NUM_BUCKETS - 1][None, :], precision=lax.Precision.HIGHEST)
    tbl = jnp.where((dist >= MAX_DISTANCE)[:, None], 0.0, tbl).T
    by_delta0 = jnp.concatenate([jnp.zeros((n_heads, tile - 1), F32), tbl[:, :tile]], axis=1)
    by_delta1 = tbl[:, 1:]
    bias_tiles = jnp.stack([_toeplitz(by_delta0, tile), _toeplitz(by_delta1, tile)], axis=0) * LOG2E

    o = sparse_attention(proj, mask_bias, bias_tiles, batch, seq, n_heads)
    return matmul_residual(o, w_o.astype(BF16), jnp.zeros((1, d), F32), x)


def conv_layer(x, norm_g, w_pw1, b_pw1, w_dw, b_dw, ln_g, ln_b, w_pw2, b_pw2, seq):
    t, d = x.shape
    u = norm_matmul_glu(x, norm_g.reshape(1, d), w_pw1.astype(BF16), b_pw1.reshape(1, 2 * d))
    fold = (CONV_FOLD, d // CONV_FOLD)
    w_dw2 = jnp.pad(w_dw[:, 0, :], ((0, CONV_HALO - CONV_WIDTH), (0, 0))).reshape(CONV_HALO, *fold)
    v = dwconv_ln_swish(u.reshape(t, *fold), w_dw2, b_dw.reshape(1, *fold), ln_g.reshape(1, *fold),
                        ln_b.reshape(1, *fold), seq)
    return matmul_residual(v.reshape(t, d).astype(BF16), w_pw2.astype(BF16), b_pw2.reshape(1, d), x)


def kernel(x, norm_mix, norm_ffn, final_norm, conv_w_pw1, conv_b_pw1, conv_w_dw, conv_b_dw, conv_ln_g,
           conv_ln_b, conv_w_pw2, conv_b_pw2, attn_w_in, attn_w_o, rel_bias, router_w, router_b, moe_w_gu,
           moe_b_gu, moe_w_down, moe_b_down):
    batch, seq, d = x.shape
    depth = norm_mix.shape[0]
    h = x.reshape(batch * seq, d)
    for i in range(depth):
        jdx = i // 2
        if i % 2 == 0:
            h = conv_layer(h, norm_mix[i], conv_w_pw1[jdx], conv_b_pw1[jdx], conv_w_dw[jdx], conv_b_dw[jdx],
                           conv_ln_g[jdx], conv_ln_b[jdx], conv_w_pw2[jdx], conv_b_pw2[jdx], seq)
        else:
            h = attention_layer(h, norm_mix[i], attn_w_in[jdx], attn_w_o[jdx], rel_bias, batch, seq)
        h = moe_layer(h, norm_ffn[i], router_w[i], router_b[i], i, moe_w_gu, moe_b_gu, moe_w_down, moe_b_down,
                      final_g=final_norm if i == depth - 1 else None)
    return h.reshape(batch, seq, d)
```

```python
import functools
import math

import numpy as np
import jax
import jax.numpy as jnp
from jax import lax
from jax.experimental import pallas as pl
from jax.experimental.pallas import tpu as pltpu

F32 = jnp.float32
BF16 = jnp.bfloat16
I32 = jnp.int32

NORM_EPS = 1e-5
CONV_WIDTH = 31
HEAD_DIM = 128
N_KV_HEADS = 4
IDX_HEADS = 16
IDX_DIM = 64
INDEX_TOPK_MAX = 256
NUM_BUCKETS = 32
MAX_DISTANCE = 128
N_EXPERTS = 32
TOP_K = 4
SWIGLU_LIMIT = 7.0
SWIGLU_ALPHA = 1.702
MOE_BLOCK = 256
MOE_ITEM_BLOCKS = 10
MOE_TFA = 256
MOE_TNB = 256
ATT_TILE = 256
CONV_HALO = 32
CONV_FOLD = 8
NEG_BIG = -1e30
INT_MIN = -2147483648
LOG2E = math.log2(math.e)
VMEM_LIMIT = 56 * 1024 * 1024


def _cparams(sem):
    return pltpu.CompilerParams(dimension_semantics=sem, vmem_limit_bytes=VMEM_LIMIT)


def _rms(x, g):
    ms = jnp.mean(x * x, axis=-1, keepdims=True)
    return (x * lax.rsqrt(ms + NORM_EPS)) * g


def _sigmoid(x):
    return 1.0 / (1.0 + jnp.exp(-x))


def _norm_mm_kernel(x_ref, g_ref, w_ref, b_ref, s_ref, o_ref, hn_ref):
    @pl.when(pl.program_id(1) == 0)
    def _():
        hn_ref[...] = _rms(x_ref[...], g_ref[...]).astype(BF16)

    acc = jnp.dot(hn_ref[...], w_ref[...], preferred_element_type=F32)
    o_ref[...] = ((acc + b_ref[...]) * s_ref[...]).astype(o_ref.dtype)


def norm_matmul(x, g, w, b, s, out_dtype, tm=1024, tn=512):
    t, d = x.shape
    n = w.shape[1]
    tn = min(tn, n)
    return pl.pallas_call(
        _norm_mm_kernel,
        grid=(t // tm, n // tn),
        in_specs=[
            pl.BlockSpec((tm, d), lambda i, j: (i, 0)),
            pl.BlockSpec((1, d), lambda i, j: (0, 0)),
            pl.BlockSpec((d, tn), lambda i, j: (0, j)),
            pl.BlockSpec((1, tn), lambda i, j: (0, j)),
            pl.BlockSpec((1, tn), lambda i, j: (0, j)),
        ],
        out_specs=pl.BlockSpec((tm, tn), lambda i, j: (i, j)),
        out_shape=jax.ShapeDtypeStruct((t, n), out_dtype),
        scratch_shapes=[pltpu.VMEM((tm, d), BF16)],
        compiler_params=_cparams(("parallel", "arbitrary")),
        name="norm_matmul",
    )(x, g, w, b, s)


def _norm_mm_glu_kernel(x_ref, g_ref, wa_ref, wg_ref, ba_ref, bg_ref, o_ref, hn_ref):
    @pl.when(pl.program_id(1) == 0)
    def _():
        hn_ref[...] = _rms(x_ref[...], g_ref[...]).astype(BF16)

    hn = hn_ref[...]
    a = jnp.dot(hn, wa_ref[...], preferred_element_type=F32) + ba_ref[...]
    gt = jnp.dot(hn, wg_ref[...], preferred_element_type=F32) + bg_ref[...]
    o_ref[...] = (a * _sigmoid(gt)).astype(o_ref.dtype)


def norm_matmul_glu(x, g, w, b, tm=1024, tn=512):
    t, d = x.shape
    n = w.shape[1] // 2
    nj = n // tn
    return pl.pallas_call(
        _norm_mm_glu_kernel,
        grid=(t // tm, nj),
        in_specs=[
            pl.BlockSpec((tm, d), lambda i, j: (i, 0)),
            pl.BlockSpec((1, d), lambda i, j: (0, 0)),
            pl.BlockSpec((d, tn), lambda i, j: (0, j)),
            pl.BlockSpec((d, tn), lambda i, j: (0, j + nj)),
            pl.BlockSpec((1, tn), lambda i, j: (0, j)),
            pl.BlockSpec((1, tn), lambda i, j: (0, j + nj)),
        ],
        out_specs=pl.BlockSpec((tm, tn), lambda i, j: (i, j)),
        out_shape=jax.ShapeDtypeStruct((t, n), F32),
        scratch_shapes=[pltpu.VMEM((tm, d), BF16)],
        compiler_params=_cparams(("parallel", "arbitrary")),
        name="norm_pw1_glu",
    )(x, g, w, w, b, b)


def _mm_res_kernel(a_ref, w_ref, b_ref, r_ref, o_ref):
    acc = jnp.dot(a_ref[...], w_ref[...], preferred_element_type=F32)
    o_ref[...] = r_ref[...] + (acc + b_ref[...])


def matmul_residual(a, w, b, res, tm=1024, tn=512):
    t, k = a.shape
    n = w.shape[1]
    return pl.pallas_call(
        _mm_res_kernel,
        grid=(t // tm, n // tn),
        in_specs=[
            pl.BlockSpec((tm, k), lambda i, j: (i, 0)),
            pl.BlockSpec((k, tn), lambda i, j: (0, j)),
            pl.BlockSpec((1, tn), lambda i, j: (0, j)),
            pl.BlockSpec((tm, tn), lambda i, j: (i, j)),
        ],
        out_specs=pl.BlockSpec((tm, tn), lambda i, j: (i, j)),
        out_shape=jax.ShapeDtypeStruct((t, n), F32),
        compiler_params=_cparams(("parallel", "parallel")),
        name="matmul_residual",
    )(a, w, b, res)


def _dwconv_kernel(prev_ref, cur_ref, w_ref, bdw_ref, lg_ref, lb_ref, o_ref, buf_ref,
                   *, ts, tiles_per_seq, row_chunk, norm_chunk):
    i = pl.program_id(0)
    first = (i % tiles_per_seq) == 0
    buf_ref[0:CONV_HALO] = jnp.where(first, 0.0, prev_ref[...])
    buf_ref[CONV_HALO:] = cur_ref[...]
    _, cs, cl = cur_ref.shape
    inv_n = 1.0 / (cs * cl)
    shift = CONV_HALO - (CONV_WIDTH - 1)

    def row_body(rc, carry):
        r0 = pl.multiple_of(rc * row_chunk, row_chunk)
        parts = [jnp.broadcast_to(bdw_ref[...], (row_chunk, cs, cl)), jnp.zeros((row_chunk, cs, cl), F32)]
        for k in range(CONV_WIDTH):
            parts[k % 2] = parts[k % 2] + w_ref[k] * buf_ref[pl.ds(r0 + k + shift, row_chunk)]
        o_ref[pl.ds(r0, row_chunk)] = parts[0] + parts[1]
        return carry

    lax.fori_loop(0, ts // row_chunk, row_body, 0)

    def norm_body(rc, carry):
        rows = pl.ds(pl.multiple_of(rc * norm_chunk, norm_chunk), norm_chunk)
        y = o_ref[rows]
        mu = jnp.sum(y, axis=(1, 2), keepdims=True) * inv_n
        yc = y - mu
        var = jnp.sum(yc * yc, axis=(1, 2), keepdims=True) * inv_n
        z = (yc * lax.rsqrt(var + NORM_EPS)) * lg_ref[...] + lb_ref[...]
        o_ref[rows] = z * _sigmoid(z)
        return carry

    lax.fori_loop(0, ts // norm_chunk, norm_body, 0)


def dwconv_ln_swish(u, w_dw, b_dw, ln_g, ln_b, seq, ts=256):
    t, cs, cl = u.shape
    hb = ts // CONV_HALO
    kern = functools.partial(_dwconv_kernel, ts=ts, tiles_per_seq=seq // ts, row_chunk=8, norm_chunk=32)
    vec = pl.BlockSpec((1, cs, cl), lambda i: (0, 0, 0))
    return pl.pallas_call(
        kern,
        grid=(t // ts,),
        in_specs=[
            pl.BlockSpec((CONV_HALO, cs, cl), lambda i: (jnp.maximum(i * hb - 1, 0), 0, 0)),
            pl.BlockSpec((ts, cs, cl), lambda i: (i, 0, 0)),
            pl.BlockSpec((CONV_HALO, cs, cl), lambda i: (0, 0, 0)),
            vec, vec, vec,
        ],
        out_specs=pl.BlockSpec((ts, cs, cl), lambda i: (i, 0, 0)),
        out_shape=jax.ShapeDtypeStruct((t, cs, cl), F32),
        scratch_shapes=[pltpu.VMEM((ts + CONV_HALO, cs, cl), F32)],
        compiler_params=_cparams(("parallel",)),
        name="dwconv_ln_swish",
    )(u, u, w_dw, b_dw, ln_g, ln_b)


def _router_kernel(x_ref, g_ref, rw_ref, rb_ref, hn_ref, idx_ref, gate_ref, rank_ref, cnt_ref):
    @pl.when(pl.program_id(0) == 0)
    def _():
        cnt_ref[...] = jnp.zeros(cnt_ref.shape, F32)

    hn = _rms(x_ref[...], g_ref[...])
    hn_ref[...] = hn.astype(BF16)
    logits = jnp.dot(hn, rw_ref[...], preferred_element_type=F32,
                     precision=lax.Precision.HIGHEST) + rb_ref[...]
    tm, ne = logits.shape
    lane = lax.broadcasted_iota(I32, (tm, ne), 1).astype(F32)
    lane4 = lax.broadcasted_iota(I32, (tm, TOP_K), 1)
    work = logits
    vals, hits = [], []
    idx_out = jnp.zeros((tm, TOP_K), F32)
    for k in range(TOP_K):
        m = jnp.max(work, axis=-1, keepdims=True)
        idx = jnp.min(jnp.where(work == m, lane, float(ne)), axis=-1, keepdims=True)
        hit = lane == idx
        vals.append(m)
        hits.append(hit)
        idx_out = jnp.where(lane4 == k, idx, idx_out)
        work = jnp.where(hit, -jnp.inf, work)
    es = [jnp.exp(v - vals[0]) for v in vals]
    denom = es[0] + es[1] + es[2] + es[3]
    sel = jnp.zeros((tm, ne), F32)
    gate_out = jnp.zeros((tm, TOP_K), F32)
    for k in range(TOP_K):
        sel = jnp.where(hits[k], 1.0, sel)
        gate_out = jnp.where(lane4 == k, es[k] / denom, gate_out)
    r_i = lax.broadcasted_iota(I32, (tm, tm), 0)
    c_i = lax.broadcasted_iota(I32, (tm, tm), 1)
    tri = jnp.where(c_i < r_i, 1.0, 0.0).astype(BF16)
    rank_full = jnp.dot(tri, sel.astype(BF16), preferred_element_type=F32) + cnt_ref[...]
    rank_out = jnp.zeros((tm, TOP_K), F32)
    for k in range(TOP_K):
        rk = jnp.sum(jnp.where(hits[k], rank_full, 0.0), axis=-1, keepdims=True)
        rank_out = jnp.where(lane4 == k, rk, rank_out)
    cnt_ref[...] = cnt_ref[...] + jnp.sum(sel, axis=0, keepdims=True)
    idx_ref[...] = idx_out.astype(I32)
    gate_ref[...] = gate_out
    rank_ref[...] = rank_out.astype(I32)


def router(x, g, rw, rb, tm=512):
    t, d = x.shape
    ne = rw.shape[1]
    return pl.pallas_call(
        _router_kernel,
        grid=(t // tm,),
        in_specs=[
            pl.BlockSpec((tm, d), lambda i: (i, 0)),
            pl.BlockSpec((1, d), lambda i: (0, 0)),
            pl.BlockSpec((d, ne), lambda i: (0, 0)),
            pl.BlockSpec((1, ne), lambda i: (0, 0)),
        ],
        out_specs=[
            pl.BlockSpec((tm, d), lambda i: (i, 0)),
            pl.BlockSpec((tm, TOP_K), lambda i: (i, 0)),
            pl.BlockSpec((tm, TOP_K), lambda i: (i, 0)),
            pl.BlockSpec((tm, TOP_K), lambda i: (i, 0)),
            pl.BlockSpec((1, ne), lambda i: (0, 0)),
        ],
        out_shape=[
            jax.ShapeDtypeStruct((t, d), BF16),
            jax.ShapeDtypeStruct((t, TOP_K), I32),
            jax.ShapeDtypeStruct((t, TOP_K), F32),
            jax.ShapeDtypeStruct((t, TOP_K), I32),
            jax.ShapeDtypeStruct((1, ne), F32),
        ],
        compiler_params=_cparams(("arbitrary",)),
        name="moe_router",
    )(x, g, rw, rb)


def _moe_kernel(ie_ref, ib_ref, in_ref, x_ref, wg_ref, wu_ref, wd_ref, bg_ref, bu_ref, bd_ref,
                o_ref, act_ref, *, nfa):
    w = pl.program_id(0)
    j = pl.program_id(1)
    nblk = in_ref[w]
    group = 8
    group_rows = group * MOE_BLOCK
    ngroup = nblk // group

    def for_row_groups(fn):
        def body(p, carry):
            fn(pl.multiple_of(p * group_rows, group_rows), group_rows)
            return carry

        lax.fori_loop(0, ngroup, body, 0)
        rem = nblk % group
        start = ngroup * group
        for part in (4, 2, 1):
            @pl.when((rem & part) != 0)
            def _():
                done = start + (rem & ~(2 * part - 1))
                fn(pl.multiple_of(done * MOE_BLOCK, part * MOE_BLOCK), part * MOE_BLOCK)

    @pl.when((nblk > 0) & (j < nfa))
    def _():
        wg = wg_ref[0, 0].astype(BF16)
        wu = wu_ref[0, 0].astype(BF16)
        bg = bg_ref[0, 0]
        bu = bu_ref[0, 0]
        cols = pl.ds(pl.multiple_of(j * MOE_TFA, MOE_TFA), MOE_TFA)

        def gate_up(r0, nrows):
            for s0 in range(0, nrows, MOE_BLOCK):
                rows = pl.ds(r0 + s0, MOE_BLOCK)
                xr = x_ref[rows, :]
                g = jnp.dot(xr, wg, preferred_element_type=F32) + bg
                u = jnp.dot(xr, wu, preferred_element_type=F32) + bu
                g = jnp.minimum(g, SWIGLU_LIMIT)
                u = jnp.clip(u, -SWIGLU_LIMIT, SWIGLU_LIMIT)
                act = (u + 1.0) * (g * _sigmoid(g * SWIGLU_ALPHA))
                act_ref[rows, cols] = act.astype(BF16)

        for_row_groups(gate_up)

    @pl.when((nblk > 0) & (j >= nfa))
    def _():
        wd = wd_ref[0, 0].astype(BF16)
        bd = bd_ref[0, 0]

        def down(r0, nrows):
            for s0 in range(0, nrows, MOE_BLOCK):
                rows = pl.ds(r0 + s0, MOE_BLOCK)
                y = jnp.dot(act_ref[rows, :], wd, preferred_element_type=F32) + bd
                o_ref[rows, :] = y.astype(o_ref.dtype)

        for_row_groups(down)


def moe_experts(xs, item_expert, item_blk0, item_nblk, n_used, layer, w_gu, b_gu, w_down, b_down):
    p_rows, d = xs.shape
    depth, ne, _, dff2 = w_gu.shape
    dff = dff2 // 2
    nfa = dff // MOE_TFA
    nfb = d // MOE_TNB
    n_items = item_expert.shape[0]
    item_rows = MOE_ITEM_BLOCKS * MOE_BLOCK

    def ja(w, j, inb):
        return jnp.where(inb[w] > 0, jnp.minimum(j, nfa - 1), nfa - 1)

    def jb(w, j, inb):
        return jnp.where(inb[w] > 0, jnp.maximum(j - nfa, 0), nfb - 1)

    b_gu4 = b_gu.reshape(depth, ne, 1, dff2)
    b_down4 = b_down.reshape(depth, ne, 1, d)
    grid_spec = pltpu.PrefetchScalarGridSpec(
        num_scalar_prefetch=3,
        grid=(n_used, nfa + nfb),
        in_specs=[
            pl.BlockSpec((pl.Element(item_rows), pl.Element(d)),
                         lambda w, j, ie, ib, inb: (ib[w] * MOE_BLOCK, 0)),
            pl.BlockSpec((1, 1, d, MOE_TFA), lambda w, j, ie, ib, inb: (layer, ie[w], 0, ja(w, j, inb))),
            pl.BlockSpec((1, 1, d, MOE_TFA), lambda w, j, ie, ib, inb: (layer, ie[w], 0, nfa + ja(w, j, inb))),
            pl.BlockSpec((1, 1, dff, MOE_TNB), lambda w, j, ie, ib, inb: (layer, ie[w], 0, jb(w, j, inb))),
            pl.BlockSpec((1, 1, 1, MOE_TFA), lambda w, j, ie, ib, inb: (layer, ie[w], 0, ja(w, j, inb))),
            pl.BlockSpec((1, 1, 1, MOE_TFA), lambda w, j, ie, ib, inb: (layer, ie[w], 0, nfa + ja(w, j, inb))),
            pl.BlockSpec((1, 1, 1, MOE_TNB), lambda w, j, ie, ib, inb: (layer, ie[w], 0, jb(w, j, inb))),
        ],
        out_specs=pl.BlockSpec((item_rows, MOE_TNB), lambda w, j, ie, ib, inb: (w, jb(w, j, inb))),
        scratch_shapes=[pltpu.VMEM((item_rows, dff), BF16)],
    )
    return pl.pallas_call(
        functools.partial(_moe_kernel, nfa=nfa),
        grid_spec=grid_spec,
        out_shape=jax.ShapeDtypeStruct((n_items * item_rows, d), BF16),
        compiler_params=_cparams(("arbitrary", "arbitrary")),
        name="moe_experts",
    )(item_expert, item_blk0, item_nblk, xs, w_gu, w_gu, w_down, b_gu4, b_gu4, b_down4)


def _combine_kernel(*refs, final):
    y_refs = refs[:TOP_K]
    g_ref, x_ref = refs[TOP_K], refs[TOP_K + 1]
    o_ref = refs[-1]
    g = g_ref[...]
    acc = x_ref[...]
    for k in range(TOP_K):
        acc = acc + g[:, k:k + 1] * y_refs[k][...].astype(F32)
    if final:
        acc = _rms(acc, refs[TOP_K + 2][...])
    o_ref[...] = acc


def moe_combine(yk, gate4, x, final_g=None, tm=256):
    t, d = x.shape
    final = final_g is not None
    nt = t // tm

    def y_map(k):
        return lambda i: (k * nt + i, 0)

    in_specs = [pl.BlockSpec((tm, d), y_map(k)) for k in range(TOP_K)] + [
        pl.BlockSpec((tm, TOP_K), lambda i: (i, 0)),
        pl.BlockSpec((tm, d), lambda i: (i, 0)),
    ]
    args = [yk] * TOP_K + [gate4, x]
    if final:
        in_specs.append(pl.BlockSpec((1, d), lambda i: (0, 0)))
        args.append(final_g.reshape(1, d))
    return pl.pallas_call(
        functools.partial(_combine_kernel, final=final),
        grid=(t // tm,),
        in_specs=in_specs,
        out_specs=pl.BlockSpec((tm, d), lambda i: (i, 0)),
        out_shape=jax.ShapeDtypeStruct((t, d), F32),
        compiler_params=_cparams(("parallel",)),
        name="moe_combine",
    )(*args)


def _lookup(table, idx):
    onehot = idx[..., None] == jnp.arange(table.shape[0], dtype=I32)
    return jnp.sum(jnp.where(onehot, table, 0), axis=-1)


def moe_layer(x, norm_g, router_w, router_b, layer, w_gu, b_gu, w_down, b_down, final_g=None):
    t, d = x.shape
    ne = router_w.shape[1]
    hn, top_idx, gate4, rank4, cnt = router(x, norm_g.reshape(1, d), router_w, router_b.reshape(1, ne))

    n_assign = t * TOP_K
    n_blocks = -(-n_assign // MOE_BLOCK) + ne
    p_rows = n_blocks * MOE_BLOCK
    item_rows = MOE_ITEM_BLOCKS * MOE_BLOCK
    counts = cnt[0].astype(I32)
    nb = (counts + MOE_BLOCK - 1) // MOE_BLOCK
    blk_start = jnp.cumsum(nb) - nb

    max_items = n_blocks // MOE_ITEM_BLOCKS + ne
    n_it = (nb + MOE_ITEM_BLOCKS - 1) // MOE_ITEM_BLOCKS
    it_cum = jnp.cumsum(n_it)
    it_start = it_cum - n_it
    total_items = it_cum[-1]
    base_e = nb // jnp.maximum(n_it, 1)
    rem_e = nb % jnp.maximum(n_it, 1)
    wids = jnp.arange(max_items, dtype=I32)
    e_of = jnp.minimum(jnp.searchsorted(it_cum, wids, side="right"), ne - 1).astype(I32)
    local = wids - it_start[e_of]
    size = base_e[e_of] + (local < rem_e[e_of]).astype(I32)
    off = local * base_e[e_of] + jnp.minimum(local, rem_e[e_of])
    valid = wids < total_items
    last_e = e_of[jnp.maximum(total_items - 1, 0)]
    item_expert = jnp.where(valid, e_of, last_e).astype(I32)
    item_nblk = jnp.where(valid, size, 0).astype(I32)
    item_blk0 = jnp.where(valid, blk_start[e_of] + off, 0).astype(I32)

    a_e = top_idx.T.reshape(-1)
    a_rank = rank4.T.reshape(-1)
    blk_in_e = a_rank // MOE_BLOCK
    within = a_rank % MOE_BLOCK
    a_base = _lookup(base_e, a_e)
    a_rem = _lookup(rem_e, a_e)
    dest = (_lookup(blk_start, a_e) + blk_in_e) * MOE_BLOCK + within
    big = a_base + 1
    n_big = a_rem * big
    in_big = blk_in_e < n_big
    num = jnp.where(in_big, blk_in_e, blk_in_e - n_big)
    den = jnp.where(in_big, big, jnp.maximum(a_base, 1))
    quo = jnp.floor((num.astype(F32) + 0.5) / den.astype(F32)).astype(I32)
    a_local = jnp.where(in_big, 0, a_rem) + quo
    a_slot = num - quo * den
    out_row = ((_lookup(it_start, a_e) + a_local) * MOE_ITEM_BLOCKS + a_slot) * MOE_BLOCK + within

    n_rows = p_rows + item_rows
    row_token = (jnp.arange(n_rows, dtype=I32) % t).at[dest].set(jnp.tile(jnp.arange(t, dtype=I32), TOP_K))
    xs = hn[row_token]
    ys = moe_experts(xs, item_expert, item_blk0, item_nblk, total_items.astype(I32), layer, w_gu, b_gu,
                     w_down, b_down)
    yk = ys[out_row]
    return moe_combine(yk, gate4, x, final_g)


def _indexer_kernel(qi_ref, kit_ref, wi_ref, o_ref, key_ref, wb_ref, cnt_ref, *, tq, topk):
    i = pl.program_id(1)
    nch = i + 1
    o_ref[...] = jnp.full(o_ref.shape, NEG_BIG, o_ref.dtype)
    row = lax.broadcasted_iota(I32, (tq, tq), 0)
    col = lax.broadcasted_iota(I32, (tq, tq), 1)
    wi = wi_ref[...]
    for h in range(IDX_HEADS):
        wb_ref[h] = jnp.broadcast_to(wi[:, h:h + 1], (tq, 128))

    def chunk_slice(c):
        return pl.ds(pl.multiple_of(c * tq, tq), tq)

    def score_chunk(c, carry):
        cs = chunk_slice(c)
        kc = kit_ref[0, :, cs]
        acc = jnp.zeros((tq, tq), F32)
        for h in range(IDX_HEADS):
            s = jnp.dot(qi_ref[:, h * IDX_DIM:(h + 1) * IDX_DIM], kc, preferred_element_type=F32)
            wh = wb_ref[h]
            acc = acc + jnp.concatenate([wh] * (tq // 128), axis=1) * jnp.maximum(s, 0.0)
        bits = pltpu.bitcast(acc, I32)
        key = bits ^ ((bits >> 31) & 0x7FFFFFFF)
        key = jnp.where((c < i) | (col <= row), key, INT_MIN)
        key_ref[:, cs] = key
        return carry

    lax.fori_loop(0, nch, score_chunk, 0)

    pos = i * tq + lax.broadcasted_iota(I32, (tq, 128), 0)
    kk = jnp.minimum(pos + 1, topk).astype(F32)

    @pl.when(nch % 2 == 1)
    def _():
        key_ref[:, chunk_slice(nch)] = jnp.full((tq, tq), INT_MIN, I32)

    cnt_rows = 64
    cnt_cols = 2 * tq

    ones = jnp.ones((128, 128), BF16)

    def body(it, v):
        cand = v | jnp.left_shift(jnp.int32(1), 31 - it)
        thr = cand ^ INT_MIN
        for r0 in range(0, tq, cnt_rows):
            thr_r = thr[r0:r0 + cnt_rows]

            def cnt_step(c, acc):
                kch = key_ref[r0:r0 + cnt_rows, pl.ds(pl.multiple_of(c * cnt_cols, cnt_cols), cnt_cols)]
                for q in range(cnt_cols // 128):
                    acc = acc + jnp.where(kch[:, q * 128:(q + 1) * 128] >= thr_r, 1.0, 0.0)
                return acc

            cnt_ref[r0:r0 + cnt_rows, :] = lax.fori_loop(0, (nch + 1) // 2, cnt_step,
                                                         jnp.zeros((cnt_rows, 128), F32))
        cnt = jnp.dot(cnt_ref[...].astype(BF16), ones, preferred_element_type=F32)
        return jnp.where(cnt >= kk, cand, v)

    v = lax.fori_loop(0, 32, body, jnp.zeros((tq, 128), I32))
    thr = jnp.concatenate([v ^ INT_MIN] * (tq // 128), axis=1)

    def out_chunk(c, carry):
        cs = chunk_slice(c)
        o_ref[:, cs] = jnp.where(key_ref[:, cs] >= thr, 0.0, NEG_BIG).astype(o_ref.dtype)
        return carry

    lax.fori_loop(0, nch, out_chunk, 0)


def indexer_mask(proj, kit, wi, batch, seq, qi_col_block):
    t = proj.shape[0]
    tq = ATT_TILE
    nq = seq // tq
    qi_cols = IDX_HEADS * IDX_DIM
    topk = min(INDEX_TOPK_MAX, seq // 4)
    kern = functools.partial(_indexer_kernel, tq=tq, topk=topk)
    return pl.pallas_call(
        kern,
        grid=(batch, nq),
        in_specs=[
            pl.BlockSpec((tq, qi_cols), lambda b, i: (b * nq + i, qi_col_block)),
            pl.BlockSpec((1, IDX_DIM, seq), lambda b, i: (b, 0, 0)),
            pl.BlockSpec((tq, IDX_HEADS), lambda b, i: (b * nq + i, 0)),
        ],
        out_specs=pl.BlockSpec((tq, seq), lambda b, i: (b * nq + i, 0)),
        out_shape=jax.ShapeDtypeStruct((t, seq), BF16),
        scratch_shapes=[pltpu.VMEM((tq, seq), I32), pltpu.VMEM((IDX_HEADS, tq, 128), F32),
                        pltpu.VMEM((tq, 128), F32)],
        compiler_params=_cparams(("parallel", "parallel")),
        name="dsa_indexer",
    )(proj, kit, wi)


def _attn_kernel(qt_ref, kt_ref, q_ref, k_ref, v_ref, mb_ref, bias_ref, o_ref, acc_ref, m_ref,
                 *, n_heads, group):
    i = qt_ref[pl.program_id(1)]
    j = kt_ref[pl.program_id(1)]
    hd = HEAD_DIM
    tq, tk = mb_ref.shape

    @pl.when(j == 0)
    def _():
        acc_ref[...] = jnp.zeros(acc_ref.shape, F32)
        m_ref[...] = jnp.full(m_ref.shape, NEG_BIG, F32)

    def heads(near):
        mb = mb_ref[...].astype(F32)
        off = i - j
        ones = jnp.ones((tk, 128), BF16)
        v_ext = [jnp.concatenate([v_ref[:, n * hd:(n + 1) * hd], ones], axis=1) for n in range(n_heads // group)]
        for h in range(n_heads):
            n = h // group
            qh = q_ref[:, h * hd:(h + 1) * hd]
            kn = k_ref[:, n * hd:(n + 1) * hd]
            s = lax.dot_general(qh, kn, (((1,), (1,)), ((), ())), preferred_element_type=F32)
            if near:
                s = s + bias_ref[off, h]
            s = s + mb
            m_prev = m_ref[h]
            m_cur = jnp.max(s, axis=1, keepdims=True)
            m_next = jnp.maximum(m_prev, m_cur)
            alpha = jnp.exp2(m_prev - m_next)
            p = jnp.exp2(s - jnp.concatenate([m_next] * (tk // 128), axis=1))
            m_ref[h] = m_next
            pv = jnp.dot(p.astype(BF16), v_ext[n], preferred_element_type=F32)
            acc_ref[h] = acc_ref[h] * jnp.concatenate([alpha, alpha], axis=1) + pv

    @pl.when(i - j < 2)
    def _():
        heads(True)

    @pl.when(i - j >= 2)
    def _():
        heads(False)

    @pl.when(j == i)
    def _():
        for h in range(n_heads):
            a = acc_ref[h]
            o_ref[:, h * hd:(h + 1) * hd] = (a[:, :hd] / a[:, hd:]).astype(o_ref.dtype)


def sparse_attention(proj, mask_bias, bias_tiles, batch, seq, n_heads):
    t = proj.shape[0]
    tq = tk = ATT_TILE
    nq = seq // tq
    q_cols = n_heads * HEAD_DIM
    kv_cols = N_KV_HEADS * HEAD_DIM
    k_blk = q_cols // kv_cols
    kern = functools.partial(_attn_kernel, n_heads=n_heads, group=n_heads // N_KV_HEADS)
    pairs = [(i, j) for i in range(nq) for j in range(i + 1)]
    q_tile = jnp.asarray([p[0] for p in pairs], I32)
    k_tile = jnp.asarray([p[1] for p in pairs], I32)
    grid_spec = pltpu.PrefetchScalarGridSpec(
        num_scalar_prefetch=2,
        grid=(batch, len(pairs)),
        in_specs=[
            pl.BlockSpec((tq, q_cols), lambda b, p, qt, kt: (b * nq + qt[p], 0)),
            pl.BlockSpec((tk, kv_cols), lambda b, p, qt, kt: (b * nq + kt[p], k_blk)),
            pl.BlockSpec((tk, kv_cols), lambda b, p, qt, kt: (b * nq + kt[p], k_blk + 1)),
            pl.BlockSpec((tq, tk), lambda b, p, qt, kt: (b * nq + qt[p], kt[p])),
            pl.BlockSpec((2, n_heads, tq, tk), lambda b, p, qt, kt: (0, 0, 0, 0)),
        ],
        out_specs=pl.BlockSpec((tq, q_cols), lambda b, p, qt, kt: (b * nq + qt[p], 0)),
        scratch_shapes=[
            pltpu.VMEM((n_heads, tq, 2 * HEAD_DIM), F32),
            pltpu.VMEM((n_heads, tq, 128), F32),
        ],
    )
    return pl.pallas_call(
        kern,
        grid_spec=grid_spec,
        out_shape=jax.ShapeDtypeStruct((t, q_cols), BF16),
        compiler_params=_cparams(("parallel", "arbitrary")),
        name="dsa_attention",
    )(q_tile, k_tile, proj, proj, proj, mask_bias, bias_tiles)


def _t5_bucket(n):
    n = jnp.maximum(n, 0)
    max_exact = NUM_BUCKETS // 2
    nf = jnp.maximum(n, 1).astype(F32)
    large = max_exact + (jnp.log(nf / max_exact) / math.log(MAX_DISTANCE / max_exact)
                         * (NUM_BUCKETS - max_exact)).astype(I32)
    large = jnp.minimum(large, NUM_BUCKETS - 1)
    return jnp.where(n < max_exact, n, large)


def _toeplitz(by_delta, n):
    h = by_delta.shape[0]
    u = jnp.concatenate([by_delta[:, :n][:, ::-1], jnp.zeros((h, 2), by_delta.dtype),
                         by_delta[:, n:][:, ::-1]], axis=1)
    rows = jnp.tile(u, (1, n))[:, :n * 2 * n].reshape(h, n, 2 * n)
    return rows[:, :, :n]


def attention_layer(x, norm_g, w_in, w_o, rel_bias, batch, seq):
    t, d = x.shape
    n_heads = w_o.shape[0] // HEAD_DIM
    q_cols = n_heads * HEAD_DIM
    kv_cols = N_KV_HEADS * HEAD_DIM
    qi_cols = IDX_HEADS * IDX_DIM
    main_cols = q_cols + 2 * kv_cols + qi_cols
    assert main_cols % qi_cols == 0 and q_cols % kv_cols == 0
    tail_cols = 128
    w_main = w_in[:, :main_cols].astype(BF16)
    w_tail = jnp.pad(w_in[:, main_cols:], ((0, 0), (0, tail_cols - (IDX_DIM + IDX_HEADS)))).astype(BF16)
    s_main = jnp.concatenate([jnp.full((q_cols,), (HEAD_DIM ** -0.5) * LOG2E, F32),
                              jnp.ones((main_cols - q_cols,), F32)]).reshape(1, main_cols)
    s_tail = jnp.concatenate([jnp.ones((IDX_DIM,), F32),
                              jnp.full((IDX_HEADS,), (IDX_HEADS ** -0.5) * (IDX_DIM ** -0.5), F32),
                              jnp.ones((tail_cols - IDX_DIM - IDX_HEADS,), F32)]).reshape(1, tail_cols)
    g2 = norm_g.reshape(1, d)
    proj = norm_matmul(x, g2, w_main, jnp.zeros((1, main_cols), F32), s_main, BF16)
    tail = norm_matmul(x, g2, w_tail, jnp.zeros((1, tail_cols), F32), s_tail, F32)
    kit = tail[:, :IDX_DIM].astype(BF16).reshape(batch, seq, IDX_DIM).transpose(0, 2, 1)
    wi = tail[:, IDX_DIM:IDX_DIM + IDX_HEADS]
    mask_bias = indexer_mask(proj, kit, wi, batch, seq, (q_cols + 2 * kv_cols) // qi_cols)

    tile = ATT_TILE
    assert tile >= MAX_DISTANCE
    dist = jnp.arange(2 * tile, dtype=I32)
    onehot = (_t5_bucket(dist)[:, None] == jnp.arange(NUM_BUCKETS, dtype=I32)).astype(F32)
    tbl = jnp.dot(onehot, rel_bias - rel_bias[NUM_BUCKETS - 1][None, :], precision=lax.Precision.HIGHEST)
    tbl = jnp.where((dist >= MAX_DISTANCE)[:, None], 0.0, tbl).T
    by_delta0 = jnp.concatenate([jnp.zeros((n_heads, tile - 1), F32), tbl[:, :tile]], axis=1)
    by_delta1 = tbl[:, 1:]
    bias_tiles = jnp.stack([_toeplitz(by_delta0, tile), _toeplitz(by_delta1, tile)], axis=0) * LOG2E

    o = sparse_attention(proj, mask_bias, bias_tiles, batch, seq, n_heads)
    return matmul_residual(o, w_o.astype(BF16), jnp.zeros((1, d), F32), x)


def conv_layer(x, norm_g, w_pw1, b_pw1, w_dw, b_dw, ln_g, ln_b, w_pw2, b_pw2, seq):
    t, d = x.shape
    u = norm_matmul_glu(x, norm_g.reshape(1, d), w_pw1.astype(BF16), b_pw1.reshape(1, 2 * d))
    fold = (CONV_FOLD, d // CONV_FOLD)
    w_dw2 = jnp.pad(w_dw[:, 0, :], ((0, CONV_HALO - CONV_WIDTH), (0, 0))).reshape(CONV_HALO, *fold)
    v = dwconv_ln_swish(u.reshape(t, *fold), w_dw2, b_dw.reshape(1, *fold), ln_g.reshape(1, *fold),
                        ln_b.reshape(1, *fold), seq)
    return matmul_residual(v.reshape(t, d).astype(BF16), w_pw2.astype(BF16), b_pw2.reshape(1, d), x)


def kernel(x, norm_mix, norm_ffn, final_norm, conv_w_pw1, conv_b_pw1, conv_w_dw, conv_b_dw, conv_ln_g,
           conv_ln_b, conv_w_pw2, conv_b_pw2, attn_w_in, attn_w_o, rel_bias, router_w, router_b, moe_w_gu,
           moe_b_gu, moe_w_down, moe_b_down):
    batch, seq, d = x.shape
    depth = norm_mix.shape[0]
    h = x.reshape(batch * seq, d)
    for i in range(depth):
        jdx = i // 2
        if i % 2 == 0:
            h = conv_layer(h, norm_mix[i], conv_w_pw1[jdx], conv_b_pw1[jdx], conv_w_dw[jdx], conv_b_dw[jdx],
                           conv_ln_g[jdx], conv_ln_b[jdx], conv_w_pw2[jdx], conv_b_pw2[jdx], seq)
        else:
            h = attention_layer(h, norm_mix[i], attn_w_in[jdx], attn_w_o[jdx], rel_bias, batch, seq)
        h = moe_layer(h, norm_ffn[i], router_w[i], router_b[i], i, moe_w_gu, moe_b_gu, moe_w_down, moe_b_down,
                      final_g=final_norm if i == depth - 1 else None)
    return h.reshape(batch, seq, d)
```

```python
import functools
import math

import numpy as np
import jax
import jax.numpy as jnp
from jax import lax
from jax.experimental import pallas as pl
from jax.experimental.pallas import tpu as pltpu

F32 = jnp.float32
BF16 = jnp.bfloat16
I32 = jnp.int32

NORM_EPS = 1e-5
CONV_WIDTH = 31
HEAD_DIM = 128
N_KV_HEADS = 4
IDX_HEADS = 16
IDX_DIM = 64
INDEX_TOPK_MAX = 256
NUM_BUCKETS = 32
MAX_DISTANCE = 128
N_EXPERTS = 32
TOP_K = 4
SWIGLU_LIMIT = 7.0
SWIGLU_ALPHA = 1.702
MOE_BLOCK = 256
MOE_ITEM_BLOCKS = 10
MOE_TFA = 512
MOE_TNB = 512
ATT_TILE = 256
CONV_HALO = 32
CONV_FOLD = 8
NEG_BIG = -1e30
INT_MIN = -2147483648
LOG2E = math.log2(math.e)
VMEM_LIMIT = 56 * 1024 * 1024


def _cparams(sem):
    return pltpu.CompilerParams(dimension_semantics=sem, vmem_limit_bytes=VMEM_LIMIT)


def _rms(x, g):
    ms = jnp.mean(x * x, axis=-1, keepdims=True)
    return (x * lax.rsqrt(ms + NORM_EPS)) * g


def _sigmoid(x):
    return 1.0 / (1.0 + jnp.exp(-x))


def _norm_mm_kernel(x_ref, g_ref, w_ref, b_ref, s_ref, o_ref, hn_ref):
    @pl.when(pl.program_id(1) == 0)
    def _():
        hn_ref[...] = _rms(x_ref[...], g_ref[...]).astype(BF16)

    acc = jnp.dot(hn_ref[...], w_ref[...], preferred_element_type=F32)
    o_ref[...] = ((acc + b_ref[...]) * s_ref[...]).astype(o_ref.dtype)


def norm_matmul(x, g, w, b, s, out_dtype, tm=1024, tn=512):
    t, d = x.shape
    n = w.shape[1]
    tn = min(tn, n)
    return pl.pallas_call(
        _norm_mm_kernel,
        grid=(t // tm, n // tn),
        in_specs=[
            pl.BlockSpec((tm, d), lambda i, j: (i, 0)),
            pl.BlockSpec((1, d), lambda i, j: (0, 0)),
            pl.BlockSpec((d, tn), lambda i, j: (0, j)),
            pl.BlockSpec((1, tn), lambda i, j: (0, j)),
            pl.BlockSpec((1, tn), lambda i, j: (0, j)),
        ],
        out_specs=pl.BlockSpec((tm, tn), lambda i, j: (i, j)),
        out_shape=jax.ShapeDtypeStruct((t, n), out_dtype),
        scratch_shapes=[pltpu.VMEM((tm, d), BF16)],
        compiler_params=_cparams(("parallel", "arbitrary")),
        name="norm_matmul",
    )(x, g, w, b, s)


def _norm_mm_glu_kernel(x_ref, g_ref, wa_ref, wg_ref, ba_ref, bg_ref, o_ref, hn_ref):
    @pl.when(pl.program_id(1) == 0)
    def _():
        hn_ref[...] = _rms(x_ref[...], g_ref[...]).astype(BF16)

    hn = hn_ref[...]
    a = jnp.dot(hn, wa_ref[...], preferred_element_type=F32) + ba_ref[...]
    gt = jnp.dot(hn, wg_ref[...], preferred_element_type=F32) + bg_ref[...]
    o_ref[...] = (a * _sigmoid(gt)).astype(o_ref.dtype)


def norm_matmul_glu(x, g, w, b, tm=1024, tn=512):
    t, d = x.shape
    n = w.shape[1] // 2
    nj = n // tn
    return pl.pallas_call(
        _norm_mm_glu_kernel,
        grid=(t // tm, nj),
        in_specs=[
            pl.BlockSpec((tm, d), lambda i, j: (i, 0)),
            pl.BlockSpec((1, d), lambda i, j: (0, 0)),
            pl.BlockSpec((d, tn), lambda i, j: (0, j)),
            pl.BlockSpec((d, tn), lambda i, j: (0, j + nj)),
            pl.BlockSpec((1, tn), lambda i, j: (0, j)),
            pl.BlockSpec((1, tn), lambda i, j: (0, j + nj)),
        ],
        out_specs=pl.BlockSpec((tm, tn), lambda i, j: (i, j)),
        out_shape=jax.ShapeDtypeStruct((t, n), F32),
        scratch_shapes=[pltpu.VMEM((tm, d), BF16)],
        compiler_params=_cparams(("parallel", "arbitrary")),
        name="norm_pw1_glu",
    )(x, g, w, w, b, b)


def _mm_res_kernel(a_ref, w_ref, b_ref, r_ref, o_ref):
    acc = jnp.dot(a_ref[...], w_ref[...], preferred_element_type=F32)
    o_ref[...] = r_ref[...] + (acc + b_ref[...])


def matmul_residual(a, w, b, res, tm=1024, tn=512):
    t, k = a.shape
    n = w.shape[1]
    return pl.pallas_call(
        _mm_res_kernel,
        grid=(t // tm, n // tn),
        in_specs=[
            pl.BlockSpec((tm, k), lambda i, j: (i, 0)),
            pl.BlockSpec((k, tn), lambda i, j: (0, j)),
            pl.BlockSpec((1, tn), lambda i, j: (0, j)),
            pl.BlockSpec((tm, tn), lambda i, j: (i, j)),
        ],
        out_specs=pl.BlockSpec((tm, tn), lambda i, j: (i, j)),
        out_shape=jax.ShapeDtypeStruct((t, n), F32),
        compiler_params=_cparams(("parallel", "parallel")),
        name="matmul_residual",
    )(a, w, b, res)


def _dwconv_kernel(prev_ref, cur_ref, w_ref, bdw_ref, lg_ref, lb_ref, o_ref, buf_ref,
                   *, ts, tiles_per_seq, row_chunk, norm_chunk):
    i = pl.program_id(0)
    first = (i % tiles_per_seq) == 0
    buf_ref[0:CONV_HALO] = jnp.where(first, 0.0, prev_ref[...])
    buf_ref[CONV_HALO:] = cur_ref[...]
    _, cs, cl = cur_ref.shape
    inv_n = 1.0 / (cs * cl)
    shift = CONV_HALO - (CONV_WIDTH - 1)

    def row_body(rc, carry):
        r0 = pl.multiple_of(rc * row_chunk, row_chunk)
        parts = [jnp.broadcast_to(bdw_ref[...], (row_chunk, cs, cl)), jnp.zeros((row_chunk, cs, cl), F32)]
        for k in range(CONV_WIDTH):
            parts[k % 2] = parts[k % 2] + w_ref[k] * buf_ref[pl.ds(r0 + k + shift, row_chunk)]
        o_ref[pl.ds(r0, row_chunk)] = parts[0] + parts[1]
        return carry

    lax.fori_loop(0, ts // row_chunk, row_body, 0)

    def norm_body(rc, carry):
        rows = pl.ds(pl.multiple_of(rc * norm_chunk, norm_chunk), norm_chunk)
        y = o_ref[rows]
        mu = jnp.sum(y, axis=(1, 2), keepdims=True) * inv_n
        yc = y - mu
        var = jnp.sum(yc * yc, axis=(1, 2), keepdims=True) * inv_n
        z = (yc * lax.rsqrt(var + NORM_EPS)) * lg_ref[...] + lb_ref[...]
        o_ref[rows] = z * _sigmoid(z)
        return carry

    lax.fori_loop(0, ts // norm_chunk, norm_body, 0)


def dwconv_ln_swish(u, w_dw, b_dw, ln_g, ln_b, seq, ts=256):
    t, cs, cl = u.shape
    hb = ts // CONV_HALO
    kern = functools.partial(_dwconv_kernel, ts=ts, tiles_per_seq=seq // ts, row_chunk=8, norm_chunk=32)
    vec = pl.BlockSpec((1, cs, cl), lambda i: (0, 0, 0))
    return pl.pallas_call(
        kern,
        grid=(t // ts,),
        in_specs=[
            pl.BlockSpec((CONV_HALO, cs, cl), lambda i: (jnp.maximum(i * hb - 1, 0), 0, 0)),
            pl.BlockSpec((ts, cs, cl), lambda i: (i, 0, 0)),
            pl.BlockSpec((CONV_HALO, cs, cl), lambda i: (0, 0, 0)),
            vec, vec, vec,
        ],
        out_specs=pl.BlockSpec((ts, cs, cl), lambda i: (i, 0, 0)),
        out_shape=jax.ShapeDtypeStruct((t, cs, cl), F32),
        scratch_shapes=[pltpu.VMEM((ts + CONV_HALO, cs, cl), F32)],
        compiler_params=_cparams(("parallel",)),
        name="dwconv_ln_swish",
    )(u, u, w_dw, b_dw, ln_g, ln_b)


def _router_kernel(x_ref, g_ref, rw_ref, rb_ref, hn_ref, idx_ref, gate_ref, rank_ref, cnt_ref):
    @pl.when(pl.program_id(0) == 0)
    def _():
        cnt_ref[...] = jnp.zeros(cnt_ref.shape, F32)

    hn = _rms(x_ref[...], g_ref[...])
    hn_ref[...] = hn.astype(BF16)
    logits = jnp.dot(hn, rw_ref[...], preferred_element_type=F32,
                     precision=lax.Precision.HIGHEST) + rb_ref[...]
    tm, ne = logits.shape
    lane = lax.broadcasted_iota(I32, (tm, ne), 1).astype(F32)
    lane4 = lax.broadcasted_iota(I32, (tm, TOP_K), 1)
    work = logits
    vals, hits = [], []
    idx_out = jnp.zeros((tm, TOP_K), F32)
    for k in range(TOP_K):
        m = jnp.max(work, axis=-1, keepdims=True)
        idx = jnp.min(jnp.where(work == m, lane, float(ne)), axis=-1, keepdims=True)
        hit = lane == idx
        vals.append(m)
        hits.append(hit)
        idx_out = jnp.where(lane4 == k, idx, idx_out)
        work = jnp.where(hit, -jnp.inf, work)
    es = [jnp.exp(v - vals[0]) for v in vals]
    denom = es[0] + es[1] + es[2] + es[3]
    sel = jnp.zeros((tm, ne), F32)
    gate_out = jnp.zeros((tm, TOP_K), F32)
    for k in range(TOP_K):
        sel = jnp.where(hits[k], 1.0, sel)
        gate_out = jnp.where(lane4 == k, es[k] / denom, gate_out)
    r_i = lax.broadcasted_iota(I32, (tm, tm), 0)
    c_i = lax.broadcasted_iota(I32, (tm, tm), 1)
    tri = jnp.where(c_i < r_i, 1.0, 0.0).astype(BF16)
    rank_full = jnp.dot(tri, sel.astype(BF16), preferred_element_type=F32) + cnt_ref[...]
    rank_out = jnp.zeros((tm, TOP_K), F32)
    for k in range(TOP_K):
        rk = jnp.sum(jnp.where(hits[k], rank_full, 0.0), axis=-1, keepdims=True)
        rank_out = jnp.where(lane4 == k, rk, rank_out)
    cnt_ref[...] = cnt_ref[...] + jnp.sum(sel, axis=0, keepdims=True)
    idx_ref[...] = idx_out.astype(I32)
    gate_ref[...] = gate_out
    rank_ref[...] = rank_out.astype(I32)


def router(x, g, rw, rb, tm=512):
    t, d = x.shape
    ne = rw.shape[1]
    return pl.pallas_call(
        _router_kernel,
        grid=(t // tm,),
        in_specs=[
            pl.BlockSpec((tm, d), lambda i: (i, 0)),
            pl.BlockSpec((1, d), lambda i: (0, 0)),
            pl.BlockSpec((d, ne), lambda i: (0, 0)),
            pl.BlockSpec((1, ne), lambda i: (0, 0)),
        ],
        out_specs=[
            pl.BlockSpec((tm, d), lambda i: (i, 0)),
            pl.BlockSpec((tm, TOP_K), lambda i: (i, 0)),
            pl.BlockSpec((tm, TOP_K), lambda i: (i, 0)),
            pl.BlockSpec((tm, TOP_K), lambda i: (i, 0)),
            pl.BlockSpec((1, ne), lambda i: (0, 0)),
        ],
        out_shape=[
            jax.ShapeDtypeStruct((t, d), BF16),
            jax.ShapeDtypeStruct((t, TOP_K), I32),
            jax.ShapeDtypeStruct((t, TOP_K), F32),
            jax.ShapeDtypeStruct((t, TOP_K), I32),
            jax.ShapeDtypeStruct((1, ne), F32),
        ],
        compiler_params=_cparams(("arbitrary",)),
        name="moe_router",
    )(x, g, rw, rb)


def _moe_kernel(ie_ref, ib_ref, in_ref, x_ref, wg_ref, wu_ref, wd_ref, bg_ref, bu_ref, bd_ref,
                o_ref, act_ref, *, nfa):
    w = pl.program_id(0)
    j = pl.program_id(1)
    nblk = in_ref[w]
    group = 8
    group_rows = group * MOE_BLOCK
    ngroup = nblk // group

    def for_row_groups(fn):
        def body(p, carry):
            fn(pl.multiple_of(p * group_rows, group_rows), group_rows)
            return carry

        lax.fori_loop(0, ngroup, body, 0)
        rem = nblk % group
        start = ngroup * group
        for part in (4, 2, 1):
            @pl.when((rem & part) != 0)
            def _():
                done = start + (rem & ~(2 * part - 1))
                fn(pl.multiple_of(done * MOE_BLOCK, part * MOE_BLOCK), part * MOE_BLOCK)

    @pl.when((nblk > 0) & (j < nfa))
    def _():
        wg = wg_ref[0, 0].astype(BF16)
        wu = wu_ref[0, 0].astype(BF16)
        bg = bg_ref[0, 0]
        bu = bu_ref[0, 0]
        cols = pl.ds(pl.multiple_of(j * MOE_TFA, MOE_TFA), MOE_TFA)

        def gate_up(r0, nrows):
            for s0 in range(0, nrows, MOE_BLOCK):
                rows = pl.ds(r0 + s0, MOE_BLOCK)
                xr = x_ref[rows, :]
                g = jnp.dot(xr, wg, preferred_element_type=F32) + bg
                u = jnp.dot(xr, wu, preferred_element_type=F32) + bu
                g = jnp.minimum(g, SWIGLU_LIMIT)
                u = jnp.clip(u, -SWIGLU_LIMIT, SWIGLU_LIMIT)
                act = (u + 1.0) * (g * _sigmoid(g * SWIGLU_ALPHA))
                act_ref[rows, cols] = act.astype(BF16)

        for_row_groups(gate_up)

    @pl.when((nblk > 0) & (j >= nfa))
    def _():
        wd = wd_ref[0, 0].astype(BF16)
        bd = bd_ref[0, 0]

        def down(r0, nrows):
            for s0 in range(0, nrows, MOE_BLOCK):
                rows = pl.ds(r0 + s0, MOE_BLOCK)
                y = jnp.dot(act_ref[rows, :], wd, preferred_element_type=F32) + bd
                o_ref[rows, :] = y.astype(o_ref.dtype)

        for_row_groups(down)


def moe_experts(xs, item_expert, item_blk0, item_nblk, n_used, layer, w_gu, b_gu, w_down, b_down):
    p_rows, d = xs.shape
    depth, ne, _, dff2 = w_gu.shape
    dff = dff2 // 2
    nfa = dff // MOE_TFA
    nfb = d // MOE_TNB
    n_items = item_expert.shape[0]
    item_rows = MOE_ITEM_BLOCKS * MOE_BLOCK

    def ja(w, j, inb):
        return jnp.where(inb[w] > 0, jnp.minimum(j, nfa - 1), nfa - 1)

    def jb(w, j, inb):
        return jnp.where(inb[w] > 0, jnp.maximum(j - nfa, 0), nfb - 1)

    b_gu4 = b_gu.reshape(depth, ne, 1, dff2)
    b_down4 = b_down.reshape(depth, ne, 1, d)
    grid_spec = pltpu.PrefetchScalarGridSpec(
        num_scalar_prefetch=3,
        grid=(n_used, nfa + nfb),
        in_specs=[
            pl.BlockSpec((pl.Element(item_rows), pl.Element(d)),
                         lambda w, j, ie, ib, inb: (ib[w] * MOE_BLOCK, 0), pipeline_mode=pl.Buffered(1)),
            pl.BlockSpec((1, 1, d, MOE_TFA), lambda w, j, ie, ib, inb: (layer, ie[w], 0, ja(w, j, inb))),
            pl.BlockSpec((1, 1, d, MOE_TFA), lambda w, j, ie, ib, inb: (layer, ie[w], 0, nfa + ja(w, j, inb))),
            pl.BlockSpec((1, 1, dff, MOE_TNB), lambda w, j, ie, ib, inb: (layer, ie[w], 0, jb(w, j, inb))),
            pl.BlockSpec((1, 1, 1, MOE_TFA), lambda w, j, ie, ib, inb: (layer, ie[w], 0, ja(w, j, inb))),
            pl.BlockSpec((1, 1, 1, MOE_TFA), lambda w, j, ie, ib, inb: (layer, ie[w], 0, nfa + ja(w, j, inb))),
            pl.BlockSpec((1, 1, 1, MOE_TNB), lambda w, j, ie, ib, inb: (layer, ie[w], 0, jb(w, j, inb))),
        ],
        out_specs=pl.BlockSpec((item_rows, MOE_TNB), lambda w, j, ie, ib, inb: (w, jb(w, j, inb))),
        scratch_shapes=[pltpu.VMEM((item_rows, dff), BF16)],
    )
    return pl.pallas_call(
        functools.partial(_moe_kernel, nfa=nfa),
        grid_spec=grid_spec,
        out_shape=jax.ShapeDtypeStruct((n_items * item_rows, d), BF16),
        compiler_params=_cparams(("arbitrary", "arbitrary")),
        name="moe_experts",
    )(item_expert, item_blk0, item_nblk, xs, w_gu, w_gu, w_down, b_gu4, b_gu4, b_down4)


def _combine_kernel(*refs, final):
    y_refs = refs[:TOP_K]
    g_ref, x_ref = refs[TOP_K], refs[TOP_K + 1]
    o_ref = refs[-1]
    g = g_ref[...]
    acc = x_ref[...]
    for k in range(TOP_K):
        acc = acc + g[:, k:k + 1] * y_refs[k][...].astype(F32)
    if final:
        acc = _rms(acc, refs[TOP_K + 2][...])
    o_ref[...] = acc


def moe_combine(yk, gate4, x, final_g=None, tm=256):
    t, d = x.shape
    final = final_g is not None
    nt = t // tm

    def y_map(k):
        return lambda i: (k * nt + i, 0)

    in_specs = [pl.BlockSpec((tm, d), y_map(k)) for k in range(TOP_K)] + [
        pl.BlockSpec((tm, TOP_K), lambda i: (i, 0)),
        pl.BlockSpec((tm, d), lambda i: (i, 0)),
    ]
    args = [yk] * TOP_K + [gate4, x]
    if final:
        in_specs.append(pl.BlockSpec((1, d), lambda i: (0, 0)))
        args.append(final_g.reshape(1, d))
    return pl.pallas_call(
        functools.partial(_combine_kernel, final=final),
        grid=(t // tm,),
        in_specs=in_specs,
        out_specs=pl.BlockSpec((tm, d), lambda i: (i, 0)),
        out_shape=jax.ShapeDtypeStruct((t, d), F32),
        compiler_params=_cparams(("parallel",)),
        name="moe_combine",
    )(*args)


def _lookup(table, idx):
    onehot = idx[..., None] == jnp.arange(table.shape[0], dtype=I32)
    return jnp.sum(jnp.where(onehot, table, 0), axis=-1)


def moe_layer(x, norm_g, router_w, router_b, layer, w_gu, b_gu, w_down, b_down, final_g=None):
    t, d = x.shape
    ne = router_w.shape[1]
    hn, top_idx, gate4, rank4, cnt = router(x, norm_g.reshape(1, d), router_w, router_b.reshape(1, ne))

    n_assign = t * TOP_K
    n_blocks = -(-n_assign // MOE_BLOCK) + ne
    p_rows = n_blocks * MOE_BLOCK
    item_rows = MOE_ITEM_BLOCKS * MOE_BLOCK
    counts = cnt[0].astype(I32)
    nb = (counts + MOE_BLOCK - 1) // MOE_BLOCK
    blk_start = jnp.cumsum(nb) - nb

    max_items = n_blocks // MOE_ITEM_BLOCKS + ne
    n_it = (nb + MOE_ITEM_BLOCKS - 1) // MOE_ITEM_BLOCKS
    it_cum = jnp.cumsum(n_it)
    it_start = it_cum - n_it
    total_items = it_cum[-1]
    base_e = nb // jnp.maximum(n_it, 1)
    rem_e = nb % jnp.maximum(n_it, 1)
    wids = jnp.arange(max_items, dtype=I32)
    e_of = jnp.minimum(jnp.searchsorted(it_cum, wids, side="right"), ne - 1).astype(I32)
    local = wids - it_start[e_of]
    size = base_e[e_of] + (local < rem_e[e_of]).astype(I32)
    off = local * base_e[e_of] + jnp.minimum(local, rem_e[e_of])
    valid = wids < total_items
    last_e = e_of[jnp.maximum(total_items - 1, 0)]
    item_expert = jnp.where(valid, e_of, last_e).astype(I32)
    item_nblk = jnp.where(valid, size, 0).astype(I32)
    item_blk0 = jnp.where(valid, blk_start[e_of] + off, 0).astype(I32)

    a_e = top_idx.T.reshape(-1)
    a_rank = rank4.T.reshape(-1)
    blk_in_e = a_rank // MOE_BLOCK
    within = a_rank % MOE_BLOCK
    a_base = _lookup(base_e, a_e)
    a_rem = _lookup(rem_e, a_e)
    dest = (_lookup(blk_start, a_e) + blk_in_e) * MOE_BLOCK + within
    big = a_base + 1
    n_big = a_rem * big
    in_big = blk_in_e < n_big
    num = jnp.where(in_big, blk_in_e, blk_in_e - n_big)
    den = jnp.where(in_big, big, jnp.maximum(a_base, 1))
    quo = jnp.floor((num.astype(F32) + 0.5) / den.astype(F32)).astype(I32)
    a_local = jnp.where(in_big, 0, a_rem) + quo
    a_slot = num - quo * den
    out_row = ((_lookup(it_start, a_e) + a_local) * MOE_ITEM_BLOCKS + a_slot) * MOE_BLOCK + within

    n_rows = p_rows + item_rows
    row_token = (jnp.arange(n_rows, dtype=I32) % t).at[dest].set(jnp.tile(jnp.arange(t, dtype=I32), TOP_K))
    xs = hn[row_token]
    ys = moe_experts(xs, item_expert, item_blk0, item_nblk, total_items.astype(I32), layer, w_gu, b_gu,
                     w_down, b_down)
    yk = ys[out_row]
    return moe_combine(yk, gate4, x, final_g)


def _indexer_kernel(qi_ref, kit_ref, wi_ref, o_ref, key_ref, wb_ref, cnt_ref, *, tq, topk):
    i = pl.program_id(1)
    nch = i + 1
    o_ref[...] = jnp.full(o_ref.shape, NEG_BIG, o_ref.dtype)
    row = lax.broadcasted_iota(I32, (tq, tq), 0)
    col = lax.broadcasted_iota(I32, (tq, tq), 1)
    wi = wi_ref[...]
    for h in range(IDX_HEADS):
        wb_ref[h] = jnp.broadcast_to(wi[:, h:h + 1], (tq, 128))

    def chunk_slice(c):
        return pl.ds(pl.multiple_of(c * tq, tq), tq)

    def score_chunk(c, carry):
        cs = chunk_slice(c)
        kc = kit_ref[0, :, cs]
        acc = jnp.zeros((tq, tq), F32)
        for h in range(IDX_HEADS):
            s = jnp.dot(qi_ref[:, h * IDX_DIM:(h + 1) * IDX_DIM], kc, preferred_element_type=F32)
            wh = wb_ref[h]
            acc = acc + jnp.concatenate([wh] * (tq // 128), axis=1) * jnp.maximum(s, 0.0)
        bits = pltpu.bitcast(acc, I32)
        key = bits ^ ((bits >> 31) & 0x7FFFFFFF)
        key = jnp.where((c < i) | (col <= row), key, INT_MIN)
        key_ref[:, cs] = key
        return carry

    lax.fori_loop(0, nch, score_chunk, 0)

    pos = i * tq + lax.broadcasted_iota(I32, (tq, 128), 0)
    kk = jnp.minimum(pos + 1, topk).astype(F32)

    @pl.when(nch % 2 == 1)
    def _():
        key_ref[:, chunk_slice(nch)] = jnp.full((tq, tq), INT_MIN, I32)

    cnt_rows = 64
    cnt_cols = 2 * tq

    ones = jnp.ones((128, 128), BF16)

    def body(it, v):
        cand = v | jnp.left_shift(jnp.int32(1), 31 - it)
        thr = cand ^ INT_MIN
        for r0 in range(0, tq, cnt_rows):
            thr_r = thr[r0:r0 + cnt_rows]

            def cnt_step(c, acc):
                kch = key_ref[r0:r0 + cnt_rows, pl.ds(pl.multiple_of(c * cnt_cols, cnt_cols), cnt_cols)]
                for q in range(cnt_cols // 128):
                    acc = acc + jnp.where(kch[:, q * 128:(q + 1) * 128] >= thr_r, 1.0, 0.0)
                return acc

            cnt_ref[r0:r0 + cnt_rows, :] = lax.fori_loop(0, (nch + 1) // 2, cnt_step,
                                                         jnp.zeros((cnt_rows, 128), F32))
        cnt = jnp.dot(cnt_ref[...].astype(BF16), ones, preferred_element_type=F32)
        return jnp.where(cnt >= kk, cand, v)

    v = lax.fori_loop(0, 32, body, jnp.zeros((tq, 128), I32))
    thr = jnp.concatenate([v ^ INT_MIN] * (tq // 128), axis=1)

    def out_chunk(c, carry):
        cs = chunk_slice(c)
        o_ref[:, cs] = jnp.where(key_ref[:, cs] >= thr, 0.0, NEG_BIG).astype(o_ref.dtype)
        return carry

    lax.fori_loop(0, nch, out_chunk, 0)


def indexer_mask(proj, kit, wi, batch, seq, qi_col_block):
    t = proj.shape[0]
    tq = ATT_TILE
    nq = seq // tq
    qi_cols = IDX_HEADS * IDX_DIM
    topk = min(INDEX_TOPK_MAX, seq // 4)
    kern = functools.partial(_indexer_kernel, tq=tq, topk=topk)
    return pl.pallas_call(
        kern,
        grid=(batch, nq),
        in_specs=[
            pl.BlockSpec((tq, qi_cols), lambda b, i: (b * nq + i, qi_col_block)),
            pl.BlockSpec((1, IDX_DIM, seq), lambda b, i: (b, 0, 0)),
            pl.BlockSpec((tq, IDX_HEADS), lambda b, i: (b * nq + i, 0)),
        ],
        out_specs=pl.BlockSpec((tq, seq), lambda b, i: (b * nq + i, 0)),
        out_shape=jax.ShapeDtypeStruct((t, seq), BF16),
        scratch_shapes=[pltpu.VMEM((tq, seq), I32), pltpu.VMEM((IDX_HEADS, tq, 128), F32),
                        pltpu.VMEM((tq, 128), F32)],
        compiler_params=_cparams(("parallel", "parallel")),
        name="dsa_indexer",
    )(proj, kit, wi)


def _attn_kernel(qt_ref, kt_ref, q_ref, k_ref, v_ref, mb_ref, bias_ref, o_ref, acc_ref, m_ref,
                 *, n_heads, group):
    i = qt_ref[pl.program_id(1)]
    j = kt_ref[pl.program_id(1)]
    hd = HEAD_DIM
    tq, tk = mb_ref.shape

    @pl.when(j == 0)
    def _():
        acc_ref[...] = jnp.zeros(acc_ref.shape, F32)
        m_ref[...] = jnp.full(m_ref.shape, NEG_BIG, F32)

    def heads(near):
        mb = mb_ref[...].astype(F32)
        off = i - j
        ones = jnp.ones((tk, 128), BF16)
        v_ext = [jnp.concatenate([v_ref[:, n * hd:(n + 1) * hd], ones], axis=1) for n in range(n_heads // group)]
        for h in range(n_heads):
            n = h // group
            qh = q_ref[:, h * hd:(h + 1) * hd]
            kn = k_ref[:, n * hd:(n + 1) * hd]
            s = lax.dot_general(qh, kn, (((1,), (1,)), ((), ())), preferred_element_type=F32)
            if near:
                s = s + bias_ref[off, h]
            s = s + mb
            m_prev = m_ref[h]
            m_cur = jnp.max(s, axis=1, keepdims=True)
            m_next = jnp.maximum(m_prev, m_cur)
            alpha = jnp.exp2(m_prev - m_next)
            p = jnp.exp2(s - jnp.concatenate([m_next] * (tk // 128), axis=1))
            m_ref[h] = m_next
            pv = jnp.dot(p.astype(BF16), v_ext[n], preferred_element_type=F32)
            acc_ref[h] = acc_ref[h] * jnp.concatenate([alpha, alpha], axis=1) + pv

    @pl.when(i - j < 2)
    def _():
        heads(True)

    @pl.when(i - j >= 2)
    def _():
        heads(False)

    @pl.when(j == i)
    def _():
        for h in range(n_heads):
            a = acc_ref[h]
            o_ref[:, h * hd:(h + 1) * hd] = (a[:, :hd] / a[:, hd:]).astype(o_ref.dtype)


def sparse_attention(proj, mask_bias, bias_tiles, batch, seq, n_heads):
    t = proj.shape[0]
    tq = tk = ATT_TILE
    nq = seq // tq
    q_cols = n_heads * HEAD_DIM
    kv_cols = N_KV_HEADS * HEAD_DIM
    k_blk = q_cols // kv_cols
    kern = functools.partial(_attn_kernel, n_heads=n_heads, group=n_heads // N_KV_HEADS)
    pairs = [(i, j) for i in range(nq) for j in range(i + 1)]
    q_tile = jnp.asarray([p[0] for p in pairs], I32)
    k_tile = jnp.asarray([p[1] for p in pairs], I32)
    grid_spec = pltpu.PrefetchScalarGridSpec(
        num_scalar_prefetch=2,
        grid=(batch, len(pairs)),
        in_specs=[
            pl.BlockSpec((tq, q_cols), lambda b, p, qt, kt: (b * nq + qt[p], 0)),
            pl.BlockSpec((tk, kv_cols), lambda b, p, qt, kt: (b * nq + kt[p], k_blk)),
            pl.BlockSpec((tk, kv_cols), lambda b, p, qt, kt: (b * nq + kt[p], k_blk + 1)),
            pl.BlockSpec((tq, tk), lambda b, p, qt, kt: (b * nq + qt[p], kt[p])),
            pl.BlockSpec((2, n_heads, tq, tk), lambda b, p, qt, kt: (0, 0, 0, 0)),
        ],
        out_specs=pl.BlockSpec((tq, q_cols), lambda b, p, qt, kt: (b * nq + qt[p], 0)),
        scratch_shapes=[
            pltpu.VMEM((n_heads, tq, 2 * HEAD_DIM), F32),
            pltpu.VMEM((n_heads, tq, 128), F32),
        ],
    )
    return pl.pallas_call(
        kern,
        grid_spec=grid_spec,
        out_shape=jax.ShapeDtypeStruct((t, q_cols), BF16),
        compiler_params=_cparams(("parallel", "arbitrary")),
        name="dsa_attention",
    )(q_tile, k_tile, proj, proj, proj, mask_bias, bias_tiles)


def _t5_bucket(n):
    n = jnp.maximum(n, 0)
    max_exact = NUM_BUCKETS // 2
    nf = jnp.maximum(n, 1).astype(F32)
    large = max_exact + (jnp.log(nf / max_exact) / math.log(MAX_DISTANCE / max_exact)
                         * (NUM_BUCKETS - max_exact)).astype(I32)
    large = jnp.minimum(large, NUM_BUCKETS - 1)
    return jnp.where(n < max_exact, n, large)


def _toeplitz(by_delta, n):
    h = by_delta.shape[0]
    u = jnp.concatenate([by_delta[:, :n][:, ::-1], jnp.zeros((h, 2), by_delta.dtype),
                         by_delta[:, n:][:, ::-1]], axis=1)
    rows = jnp.tile(u, (1, n))[:, :n * 2 * n].reshape(h, n, 2 * n)
    return rows[:, :, :n]


def attention_layer(x, norm_g, w_in, w_o, rel_bias, batch, seq):
    t, d = x.shape
    n_heads = w_o.shape[0] // HEAD_DIM
    q_cols = n_heads * HEAD_DIM
    kv_cols = N_KV_HEADS * HEAD_DIM
    qi_cols = IDX_HEADS * IDX_DIM
    main_cols = q_cols + 2 * kv_cols + qi_cols
    assert main_cols % qi_cols == 0 and q_cols % kv_cols == 0
    tail_cols = 128
    w_main = w_in[:, :main_cols].astype(BF16)
    w_tail = jnp.pad(w_in[:, main_cols:], ((0, 0), (0, tail_cols - (IDX_DIM + IDX_HEADS)))).astype(BF16)
    s_main = jnp.concatenate([jnp.full((q_cols,), (HEAD_DIM ** -0.5) * LOG2E, F32),
                              jnp.ones((main_cols - q_cols,), F32)]).reshape(1, main_cols)
    s_tail = jnp.concatenate([jnp.ones((IDX_DIM,), F32),
                              jnp.full((IDX_HEADS,), (IDX_HEADS ** -0.5) * (IDX_DIM ** -0.5), F32),
                              jnp.ones((tail_cols - IDX_DIM - IDX_HEADS,), F32)]).reshape(1, tail_cols)
    g2 = norm_g.reshape(1, d)
    proj = norm_matmul(x, g2, w_main, jnp.zeros((1, main_cols), F32), s_main, BF16)
    tail = norm_matmul(x, g2, w_tail, jnp.zeros((1, tail_cols), F32), s_tail, F32)
    kit = tail[:, :IDX_DIM].astype(BF16).reshape(batch, seq, IDX_DIM).transpose(0, 2, 1)
    wi = tail[:, IDX_DIM:IDX_DIM + IDX_HEADS]
    mask_bias = indexer_mask(proj, kit, wi, batch, seq, (q_cols + 2 * kv_cols) // qi_cols)

    tile = ATT_TILE
    assert tile >= MAX_DISTANCE
    dist = jnp.arange(2 * tile, dtype=I32)
    onehot = (_t5_bucket(dist)[:, None] == jnp.arange(NUM_BUCKETS, dtype=I32)).astype(F32)
    tbl = jnp.dot(onehot, rel_bias - rel_bias[NUM_BUCKETS - 1][None, :], precision=lax.Precision.HIGHEST)
    tbl = jnp.where((dist >= MAX_DISTANCE)[:, None], 0.0, tbl).T
    by_delta0 = jnp.concatenate([jnp.zeros((n_heads, tile - 1), F32), tbl[:, :tile]], axis=1)
    by_delta1 = tbl[:, 1:]
    bias_tiles = jnp.stack([_toeplitz(by_delta0, tile), _toeplitz(by_delta1, tile)], axis=0) * LOG2E

    o = sparse_attention(proj, mask_bias, bias_tiles, batch, seq, n_heads)
    return matmul_residual(o, w_o.astype(BF16), jnp.zeros((1, d), F32), x)


def conv_layer(x, norm_g, w_pw1, b_pw1, w_dw, b_dw, ln_g, ln_b, w_pw2, b_pw2, seq):
    t, d = x.shape
    u = norm_matmul_glu(x, norm_g.reshape(1, d), w_pw1.astype(BF16), b_pw1.reshape(1, 2 * d))
    fold = (CONV_FOLD, d // CONV_FOLD)
    w_dw2 = jnp.pad(w_dw[:, 0, :], ((0, CONV_HALO - CONV_WIDTH), (0, 0))).reshape(CONV_HALO, *fold)
    v = dwconv_ln_swish(u.reshape(t, *fold), w_dw2, b_dw.reshape(1, *fold), ln_g.reshape(1, *fold),
                        ln_b.reshape(1, *fold), seq)
    return matmul_residual(v.reshape(t, d).astype(BF16), w_pw2.astype(BF16), b_pw2.reshape(1, d), x)


def kernel(x, norm_mix, norm_ffn, final_norm, conv_w_pw1, conv_b_pw1, conv_w_dw, conv_b_dw, conv_ln_g,
           conv_ln_b, conv_w_pw2, conv_b_pw2, attn_w_in, attn_w_o, rel_bias, router_w, router_b, moe_w_gu,
           moe_b_gu, moe_w_down, moe_b_down):
    batch, seq, d = x.shape
    depth = norm_mix.shape[0]
    h = x.reshape(batch * seq, d)
    for i in range(depth):
        jdx = i // 2
        if i % 2 == 0:
            h = conv_layer(h, norm_mix[i], conv_w_pw1[jdx], conv_b_pw1[jdx], conv_w_dw[jdx], conv_b_dw[jdx],
                           conv_ln_g[jdx], conv_ln_b[jdx], conv_w_pw2[jdx], conv_b_pw2[jdx], seq)
        else:
            h = attention_layer(h, norm_mix[i], attn_w_in[jdx], attn_w_o[jdx], rel_bias, batch, seq)
        h = moe_layer(h, norm_ffn[i], router_w[i], router_b[i], i, moe_w_gu, moe_b_gu, moe_w_down, moe_b_down,
                      final_g=final_norm if i == depth - 1 else None)
    return h.reshape(batch, seq, d)
```

```python
import functools
import math

import numpy as np
import jax
import jax.numpy as jnp
from jax import lax
from jax.experimental import pallas as pl
from jax.experimental.pallas import tpu as pltpu

F32 = jnp.float32
BF16 = jnp.bfloat16
I32 = jnp.int32

NORM_EPS = 1e-5
CONV_WIDTH = 31
HEAD_DIM = 128
N_KV_HEADS = 4
IDX_HEADS = 16
IDX_DIM = 64
INDEX_TOPK_MAX = 256
NUM_BUCKETS = 32
MAX_DISTANCE = 128
N_EXPERTS = 32
TOP_K = 4
SWIGLU_LIMIT = 7.0
SWIGLU_ALPHA = 1.702
MOE_BLOCK = 256
MOE_ITEM_BLOCKS = 10
MOE_TFA = 512
MOE_TNB = 512
ATT_TILE = 256
IDX_TILE = 512
CONV_HALO = 32
CONV_FOLD = 8
NEG_BIG = -1e30
INT_MIN = -2147483648
LOG2E = math.log2(math.e)
VMEM_LIMIT = 56 * 1024 * 1024


def _cparams(sem):
    return pltpu.CompilerParams(dimension_semantics=sem, vmem_limit_bytes=VMEM_LIMIT)


def _rms(x, g):
    ms = jnp.mean(x * x, axis=-1, keepdims=True)
    return (x * lax.rsqrt(ms + NORM_EPS)) * g


def _sigmoid(x):
    return 1.0 / (1.0 + jnp.exp(-x))


def _norm_mm_kernel(x_ref, g_ref, w_ref, b_ref, s_ref, o_ref, hn_ref):
    @pl.when(pl.program_id(1) == 0)
    def _():
        hn_ref[...] = _rms(x_ref[...], g_ref[...]).astype(BF16)

    acc = jnp.dot(hn_ref[...], w_ref[...], preferred_element_type=F32)
    o_ref[...] = ((acc + b_ref[...]) * s_ref[...]).astype(o_ref.dtype)


def norm_matmul(x, g, w, b, s, out_dtype, tm=1024, tn=512):
    t, d = x.shape
    n = w.shape[1]
    tn = min(tn, n)
    return pl.pallas_call(
        _norm_mm_kernel,
        grid=(t // tm, n // tn),
        in_specs=[
            pl.BlockSpec((tm, d), lambda i, j: (i, 0)),
            pl.BlockSpec((1, d), lambda i, j: (0, 0)),
            pl.BlockSpec((d, tn), lambda i, j: (0, j)),
            pl.BlockSpec((1, tn), lambda i, j: (0, j)),
            pl.BlockSpec((1, tn), lambda i, j: (0, j)),
        ],
        out_specs=pl.BlockSpec((tm, tn), lambda i, j: (i, j)),
        out_shape=jax.ShapeDtypeStruct((t, n), out_dtype),
        scratch_shapes=[pltpu.VMEM((tm, d), BF16)],
        compiler_params=_cparams(("parallel", "arbitrary")),
        name="norm_matmul",
    )(x, g, w, b, s)


def _norm_mm_glu_kernel(x_ref, g_ref, wa_ref, wg_ref, ba_ref, bg_ref, o_ref, hn_ref):
    @pl.when(pl.program_id(1) == 0)
    def _():
        hn_ref[...] = _rms(x_ref[...], g_ref[...]).astype(BF16)

    hn = hn_ref[...]
    a = jnp.dot(hn, wa_ref[...], preferred_element_type=F32) + ba_ref[...]
    gt = jnp.dot(hn, wg_ref[...], preferred_element_type=F32) + bg_ref[...]
    o_ref[...] = (a * _sigmoid(gt)).astype(o_ref.dtype)


def norm_matmul_glu(x, g, w, b, tm=1024, tn=512):
    t, d = x.shape
    n = w.shape[1] // 2
    nj = n // tn
    return pl.pallas_call(
        _norm_mm_glu_kernel,
        grid=(t // tm, nj),
        in_specs=[
            pl.BlockSpec((tm, d), lambda i, j: (i, 0)),
            pl.BlockSpec((1, d), lambda i, j: (0, 0)),
            pl.BlockSpec((d, tn), lambda i, j: (0, j)),
            pl.BlockSpec((d, tn), lambda i, j: (0, j + nj)),
            pl.BlockSpec((1, tn), lambda i, j: (0, j)),
            pl.BlockSpec((1, tn), lambda i, j: (0, j + nj)),
        ],
        out_specs=pl.BlockSpec((tm, tn), lambda i, j: (i, j)),
        out_shape=jax.ShapeDtypeStruct((t, n), F32),
        scratch_shapes=[pltpu.VMEM((tm, d), BF16)],
        compiler_params=_cparams(("parallel", "arbitrary")),
        name="norm_pw1_glu",
    )(x, g, w, w, b, b)


def _mm_res_kernel(a_ref, w_ref, b_ref, r_ref, o_ref):
    acc = jnp.dot(a_ref[...], w_ref[...], preferred_element_type=F32)
    o_ref[...] = r_ref[...] + (acc + b_ref[...])


def matmul_residual(a, w, b, res, tm=1024, tn=512):
    t, k = a.shape
    n = w.shape[1]
    return pl.pallas_call(
        _mm_res_kernel,
        grid=(t // tm, n // tn),
        in_specs=[
            pl.BlockSpec((tm, k), lambda i, j: (i, 0)),
            pl.BlockSpec((k, tn), lambda i, j: (0, j)),
            pl.BlockSpec((1, tn), lambda i, j: (0, j)),
            pl.BlockSpec((tm, tn), lambda i, j: (i, j)),
        ],
        out_specs=pl.BlockSpec((tm, tn), lambda i, j: (i, j)),
        out_shape=jax.ShapeDtypeStruct((t, n), F32),
        compiler_params=_cparams(("parallel", "parallel")),
        name="matmul_residual",
    )(a, w, b, res)


def _dwconv_kernel(prev_ref, cur_ref, w_ref, bdw_ref, lg_ref, lb_ref, o_ref, buf_ref,
                   *, ts, tiles_per_seq, row_chunk, norm_chunk):
    i = pl.program_id(0)
    first = (i % tiles_per_seq) == 0
    buf_ref[0:CONV_HALO] = jnp.where(first, 0.0, prev_ref[...])
    buf_ref[CONV_HALO:] = cur_ref[...]
    _, cs, cl = cur_ref.shape
    inv_n = 1.0 / (cs * cl)
    shift = CONV_HALO - (CONV_WIDTH - 1)

    def row_body(rc, carry):
        r0 = pl.multiple_of(rc * row_chunk, row_chunk)
        parts = [jnp.broadcast_to(bdw_ref[...], (row_chunk, cs, cl)), jnp.zeros((row_chunk, cs, cl), F32)]
        for k in range(CONV_WIDTH):
            parts[k % 2] = parts[k % 2] + w_ref[k] * buf_ref[pl.ds(r0 + k + shift, row_chunk)]
        o_ref[pl.ds(r0, row_chunk)] = parts[0] + parts[1]
        return carry

    lax.fori_loop(0, ts // row_chunk, row_body, 0)

    def norm_body(rc, carry):
        rows = pl.ds(pl.multiple_of(rc * norm_chunk, norm_chunk), norm_chunk)
        y = o_ref[rows]
        mu = jnp.sum(y, axis=(1, 2), keepdims=True) * inv_n
        yc = y - mu
        var = jnp.sum(yc * yc, axis=(1, 2), keepdims=True) * inv_n
        z = (yc * lax.rsqrt(var + NORM_EPS)) * lg_ref[...] + lb_ref[...]
        o_ref[rows] = z * _sigmoid(z)
        return carry

    lax.fori_loop(0, ts // norm_chunk, norm_body, 0)


def dwconv_ln_swish(u, w_dw, b_dw, ln_g, ln_b, seq, ts=256):
    t, cs, cl = u.shape
    hb = ts // CONV_HALO
    kern = functools.partial(_dwconv_kernel, ts=ts, tiles_per_seq=seq // ts, row_chunk=8, norm_chunk=32)
    vec = pl.BlockSpec((1, cs, cl), lambda i: (0, 0, 0))
    return pl.pallas_call(
        kern,
        grid=(t // ts,),
        in_specs=[
            pl.BlockSpec((CONV_HALO, cs, cl), lambda i: (jnp.maximum(i * hb - 1, 0), 0, 0)),
            pl.BlockSpec((ts, cs, cl), lambda i: (i, 0, 0)),
            pl.BlockSpec((CONV_HALO, cs, cl), lambda i: (0, 0, 0)),
            vec, vec, vec,
        ],
        out_specs=pl.BlockSpec((ts, cs, cl), lambda i: (i, 0, 0)),
        out_shape=jax.ShapeDtypeStruct((t, cs, cl), F32),
        scratch_shapes=[pltpu.VMEM((ts + CONV_HALO, cs, cl), F32)],
        compiler_params=_cparams(("parallel",)),
        name="dwconv_ln_swish",
    )(u, u, w_dw, b_dw, ln_g, ln_b)


def _router_kernel(x_ref, g_ref, rw_ref, rb_ref, hn_ref, idx_ref, gate_ref, rank_ref, cnt_ref):
    @pl.when(pl.program_id(0) == 0)
    def _():
        cnt_ref[...] = jnp.zeros(cnt_ref.shape, F32)

    hn = _rms(x_ref[...], g_ref[...])
    hn_ref[...] = hn.astype(BF16)
    logits = jnp.dot(hn, rw_ref[...], preferred_element_type=F32,
                     precision=lax.Precision.HIGHEST) + rb_ref[...]
    tm, ne = logits.shape
    lane = lax.broadcasted_iota(I32, (tm, ne), 1).astype(F32)
    lane4 = lax.broadcasted_iota(I32, (tm, TOP_K), 1)
    work = logits
    vals, hits = [], []
    idx_out = jnp.zeros((tm, TOP_K), F32)
    for k in range(TOP_K):
        m = jnp.max(work, axis=-1, keepdims=True)
        idx = jnp.min(jnp.where(work == m, lane, float(ne)), axis=-1, keepdims=True)
        hit = lane == idx
        vals.append(m)
        hits.append(hit)
        idx_out = jnp.where(lane4 == k, idx, idx_out)
        work = jnp.where(hit, -jnp.inf, work)
    es = [jnp.exp(v - vals[0]) for v in vals]
    denom = es[0] + es[1] + es[2] + es[3]
    sel = jnp.zeros((tm, ne), F32)
    gate_out = jnp.zeros((tm, TOP_K), F32)
    for k in range(TOP_K):
        sel = jnp.where(hits[k], 1.0, sel)
        gate_out = jnp.where(lane4 == k, es[k] / denom, gate_out)
    r_i = lax.broadcasted_iota(I32, (tm, tm), 0)
    c_i = lax.broadcasted_iota(I32, (tm, tm), 1)
    tri = jnp.where(c_i < r_i, 1.0, 0.0).astype(BF16)
    rank_full = jnp.dot(tri, sel.astype(BF16), preferred_element_type=F32) + cnt_ref[...]
    rank_out = jnp.zeros((tm, TOP_K), F32)
    for k in range(TOP_K):
        rk = jnp.sum(jnp.where(hits[k], rank_full, 0.0), axis=-1, keepdims=True)
        rank_out = jnp.where(lane4 == k, rk, rank_out)
    cnt_ref[...] = cnt_ref[...] + jnp.sum(sel, axis=0, keepdims=True)
    idx_ref[...] = idx_out.astype(I32)
    gate_ref[...] = gate_out
    rank_ref[...] = rank_out.astype(I32)


def router(x, g, rw, rb, tm=512):
    t, d = x.shape
    ne = rw.shape[1]
    return pl.pallas_call(
        _router_kernel,
        grid=(t // tm,),
        in_specs=[
            pl.BlockSpec((tm, d), lambda i: (i, 0)),
            pl.BlockSpec((1, d), lambda i: (0, 0)),
            pl.BlockSpec((d, ne), lambda i: (0, 0)),
            pl.BlockSpec((1, ne), lambda i: (0, 0)),
        ],
        out_specs=[
            pl.BlockSpec((tm, d), lambda i: (i, 0)),
            pl.BlockSpec((tm, TOP_K), lambda i: (i, 0)),
            pl.BlockSpec((tm, TOP_K), lambda i: (i, 0)),
            pl.BlockSpec((tm, TOP_K), lambda i: (i, 0)),
            pl.BlockSpec((1, ne), lambda i: (0, 0)),
        ],
        out_shape=[
            jax.ShapeDtypeStruct((t, d), BF16),
            jax.ShapeDtypeStruct((t, TOP_K), I32),
            jax.ShapeDtypeStruct((t, TOP_K), F32),
            jax.ShapeDtypeStruct((t, TOP_K), I32),
            jax.ShapeDtypeStruct((1, ne), F32),
        ],
        compiler_params=_cparams(("arbitrary",)),
        name="moe_router",
    )(x, g, rw, rb)


def _moe_kernel(ie_ref, ib_ref, in_ref, x_ref, wg_ref, wu_ref, wd_ref, bg_ref, bu_ref, bd_ref,
                o_ref, act_ref, *, nfa):
    w = pl.program_id(0)
    j = pl.program_id(1)
    nblk = in_ref[w]
    group = 8
    group_rows = group * MOE_BLOCK
    ngroup = nblk // group

    def for_row_groups(fn):
        def body(p, carry):
            fn(pl.multiple_of(p * group_rows, group_rows), group_rows)
            return carry

        lax.fori_loop(0, ngroup, body, 0)
        rem = nblk % group
        start = ngroup * group
        for part in (4, 2, 1):
            @pl.when((rem & part) != 0)
            def _():
                done = start + (rem & ~(2 * part - 1))
                fn(pl.multiple_of(done * MOE_BLOCK, part * MOE_BLOCK), part * MOE_BLOCK)

    @pl.when((nblk > 0) & (j < nfa))
    def _():
        wg = wg_ref[0, 0].astype(BF16)
        wu = wu_ref[0, 0].astype(BF16)
        bg = bg_ref[0, 0]
        bu = bu_ref[0, 0]
        cols = pl.ds(pl.multiple_of(j * MOE_TFA, MOE_TFA), MOE_TFA)

        def gate_up(r0, nrows):
            for s0 in range(0, nrows, MOE_BLOCK):
                rows = pl.ds(r0 + s0, MOE_BLOCK)
                xr = x_ref[rows, :]
                g = jnp.dot(xr, wg, preferred_element_type=F32) + bg
                u = jnp.dot(xr, wu, preferred_element_type=F32) + bu
                g = jnp.minimum(g, SWIGLU_LIMIT)
                u = jnp.clip(u, -SWIGLU_LIMIT, SWIGLU_LIMIT)
                act = (u + 1.0) * (g * _sigmoid(g * SWIGLU_ALPHA))
                act_ref[rows, cols] = act.astype(BF16)

        for_row_groups(gate_up)

    @pl.when((nblk > 0) & (j >= nfa))
    def _():
        wd = wd_ref[0, 0].astype(BF16)
        bd = bd_ref[0, 0]

        def down(r0, nrows):
            for s0 in range(0, nrows, MOE_BLOCK):
                rows = pl.ds(r0 + s0, MOE_BLOCK)
                y = jnp.dot(act_ref[rows, :], wd, preferred_element_type=F32) + bd
                o_ref[rows, :] = y.astype(o_ref.dtype)

        for_row_groups(down)


def moe_experts(xs, item_expert, item_blk0, item_nblk, n_used, layer, w_gu, b_gu, w_down, b_down):
    p_rows, d = xs.shape
    depth, ne, _, dff2 = w_gu.shape
    dff = dff2 // 2
    nfa = dff // MOE_TFA
    nfb = d // MOE_TNB
    n_items = item_expert.shape[0]
    item_rows = MOE_ITEM_BLOCKS * MOE_BLOCK

    def ja(w, j, inb):
        return jnp.where(inb[w] > 0, jnp.minimum(j, nfa - 1), nfa - 1)

    def jb(w, j, inb):
        return jnp.where(inb[w] > 0, jnp.maximum(j - nfa, 0), nfb - 1)

    b_gu4 = b_gu.reshape(depth, ne, 1, dff2)
    b_down4 = b_down.reshape(depth, ne, 1, d)
    grid_spec = pltpu.PrefetchScalarGridSpec(
        num_scalar_prefetch=3,
        grid=(n_used, nfa + nfb),
        in_specs=[
            pl.BlockSpec((pl.Element(item_rows), pl.Element(d)),
                         lambda w, j, ie, ib, inb: (ib[w] * MOE_BLOCK, 0), pipeline_mode=pl.Buffered(1)),
            pl.BlockSpec((1, 1, d, MOE_TFA), lambda w, j, ie, ib, inb: (layer, ie[w], 0, ja(w, j, inb))),
            pl.BlockSpec((1, 1, d, MOE_TFA), lambda w, j, ie, ib, inb: (layer, ie[w], 0, nfa + ja(w, j, inb))),
            pl.BlockSpec((1, 1, dff, MOE_TNB), lambda w, j, ie, ib, inb: (layer, ie[w], 0, jb(w, j, inb))),
            pl.BlockSpec((1, 1, 1, MOE_TFA), lambda w, j, ie, ib, inb: (layer, ie[w], 0, ja(w, j, inb))),
            pl.BlockSpec((1, 1, 1, MOE_TFA), lambda w, j, ie, ib, inb: (layer, ie[w], 0, nfa + ja(w, j, inb))),
            pl.BlockSpec((1, 1, 1, MOE_TNB), lambda w, j, ie, ib, inb: (layer, ie[w], 0, jb(w, j, inb))),
        ],
        out_specs=pl.BlockSpec((item_rows, MOE_TNB), lambda w, j, ie, ib, inb: (w, jb(w, j, inb))),
        scratch_shapes=[pltpu.VMEM((item_rows, dff), BF16)],
    )
    return pl.pallas_call(
        functools.partial(_moe_kernel, nfa=nfa),
        grid_spec=grid_spec,
        out_shape=jax.ShapeDtypeStruct((n_items * item_rows, d), BF16),
        compiler_params=_cparams(("arbitrary", "arbitrary")),
        name="moe_experts",
    )(item_expert, item_blk0, item_nblk, xs, w_gu, w_gu, w_down, b_gu4, b_gu4, b_down4)


def _combine_kernel(*refs, final):
    y_refs = refs[:TOP_K]
    g_ref, x_ref = refs[TOP_K], refs[TOP_K + 1]
    o_ref = refs[-1]
    g = g_ref[...]
    acc = x_ref[...]
    for k in range(TOP_K):
        acc = acc + g[:, k:k + 1] * y_refs[k][...].astype(F32)
    if final:
        acc = _rms(acc, refs[TOP_K + 2][...])
    o_ref[...] = acc


def moe_combine(yk, gate4, x, final_g=None, tm=256):
    t, d = x.shape
    final = final_g is not None
    nt = t // tm

    def y_map(k):
        return lambda i: (k * nt + i, 0)

    in_specs = [pl.BlockSpec((tm, d), y_map(k)) for k in range(TOP_K)] + [
        pl.BlockSpec((tm, TOP_K), lambda i: (i, 0)),
        pl.BlockSpec((tm, d), lambda i: (i, 0)),
    ]
    args = [yk] * TOP_K + [gate4, x]
    if final:
        in_specs.append(pl.BlockSpec((1, d), lambda i: (0, 0)))
        args.append(final_g.reshape(1, d))
    return pl.pallas_call(
        functools.partial(_combine_kernel, final=final),
        grid=(t // tm,),
        in_specs=in_specs,
        out_specs=pl.BlockSpec((tm, d), lambda i: (i, 0)),
        out_shape=jax.ShapeDtypeStruct((t, d), F32),
        compiler_params=_cparams(("parallel",)),
        name="moe_combine",
    )(*args)


def _lookup(table, idx):
    onehot = idx[..., None] == jnp.arange(table.shape[0], dtype=I32)
    return jnp.sum(jnp.where(onehot, table, 0), axis=-1)


def moe_layer(x, norm_g, router_w, router_b, layer, w_gu, b_gu, w_down, b_down, final_g=None):
    t, d = x.shape
    ne = router_w.shape[1]
    hn, top_idx, gate4, rank4, cnt = router(x, norm_g.reshape(1, d), router_w, router_b.reshape(1, ne))

    n_assign = t * TOP_K
    n_blocks = -(-n_assign // MOE_BLOCK) + ne
    p_rows = n_blocks * MOE_BLOCK
    item_rows = MOE_ITEM_BLOCKS * MOE_BLOCK
    counts = cnt[0].astype(I32)
    nb = (counts + MOE_BLOCK - 1) // MOE_BLOCK
    blk_start = jnp.cumsum(nb) - nb

    max_items = n_blocks // MOE_ITEM_BLOCKS + ne
    n_it = (nb + MOE_ITEM_BLOCKS - 1) // MOE_ITEM_BLOCKS
    it_cum = jnp.cumsum(n_it)
    it_start = it_cum - n_it
    total_items = it_cum[-1]
    base_e = nb // jnp.maximum(n_it, 1)
    rem_e = nb % jnp.maximum(n_it, 1)
    wids = jnp.arange(max_items, dtype=I32)
    e_of = jnp.minimum(jnp.searchsorted(it_cum, wids, side="right"), ne - 1).astype(I32)
    local = wids - it_start[e_of]
    size = base_e[e_of] + (local < rem_e[e_of]).astype(I32)
    off = local * base_e[e_of] + jnp.minimum(local, rem_e[e_of])
    valid = wids < total_items
    last_e = e_of[jnp.maximum(total_items - 1, 0)]
    item_expert = jnp.where(valid, e_of, last_e).astype(I32)
    item_nblk = jnp.where(valid, size, 0).astype(I32)
    item_blk0 = jnp.where(valid, blk_start[e_of] + off, 0).astype(I32)

    a_e = top_idx.T.reshape(-1)
    a_rank = rank4.T.reshape(-1)
    blk_in_e = a_rank // MOE_BLOCK
    within = a_rank % MOE_BLOCK
    a_base = _lookup(base_e, a_e)
    a_rem = _lookup(rem_e, a_e)
    dest = (_lookup(blk_start, a_e) + blk_in_e) * MOE_BLOCK + within
    big = a_base + 1
    n_big = a_rem * big
    in_big = blk_in_e < n_big
    num = jnp.where(in_big, blk_in_e, blk_in_e - n_big)
    den = jnp.where(in_big, big, jnp.maximum(a_base, 1))
    quo = jnp.floor((num.astype(F32) + 0.5) / den.astype(F32)).astype(I32)
    a_local = jnp.where(in_big, 0, a_rem) + quo
    a_slot = num - quo * den
    out_row = ((_lookup(it_start, a_e) + a_local) * MOE_ITEM_BLOCKS + a_slot) * MOE_BLOCK + within

    n_rows = p_rows + item_rows
    row_token = (jnp.arange(n_rows, dtype=I32) % t).at[dest].set(jnp.tile(jnp.arange(t, dtype=I32), TOP_K))
    xs = hn[row_token]
    ys = moe_experts(xs, item_expert, item_blk0, item_nblk, total_items.astype(I32), layer, w_gu, b_gu,
                     w_down, b_down)
    yk = ys[out_row]
    return moe_combine(yk, gate4, x, final_g)


def _indexer_kernel(qi_ref, kit_ref, wi_ref, o_ref, key_ref, wb_ref, cnt_ref, *, tq, cw, topk):
    i = pl.program_id(1)
    assert (tq // cw) % 2 == 0
    nch = (i + 1) * (tq // cw)
    o_ref[...] = jnp.full(o_ref.shape, NEG_BIG, o_ref.dtype)
    col_minus_row = lax.broadcasted_iota(I32, (tq, cw), 1) - lax.broadcasted_iota(I32, (tq, cw), 0)
    wi = wi_ref[...]
    for h in range(IDX_HEADS):
        wb_ref[h] = jnp.broadcast_to(wi[:, h:h + 1], (tq, 128))

    def chunk_slice(c):
        return pl.ds(pl.multiple_of(c * cw, cw), cw)

    def score_chunk(c, carry):
        cs = chunk_slice(c)
        kc = kit_ref[0, :, cs]
        acc = jnp.zeros((tq, cw), F32)
        for h in range(IDX_HEADS):
            s = jnp.dot(qi_ref[:, h * IDX_DIM:(h + 1) * IDX_DIM], kc, preferred_element_type=F32)
            wh = wb_ref[h]
            acc = acc + jnp.concatenate([wh] * (cw // 128), axis=1) * jnp.maximum(s, 0.0)
        bits = pltpu.bitcast(acc, I32)
        key = bits ^ ((bits >> 31) & 0x7FFFFFFF)
        key = jnp.where(col_minus_row <= i * tq - c * cw, key, INT_MIN)
        key_ref[:, cs] = key
        return carry

    lax.fori_loop(0, nch, score_chunk, 0)

    pos = i * tq + lax.broadcasted_iota(I32, (tq, 128), 0)
    kk = jnp.minimum(pos + 1, topk).astype(F32)

    cnt_rows = 64
    cnt_cols = 2 * cw

    ones = jnp.ones((128, 128), BF16)

    def body(it, v):
        cand = v | jnp.left_shift(jnp.int32(1), 31 - it)
        thr = cand ^ INT_MIN
        for r0 in range(0, tq, cnt_rows):
            thr_r = thr[r0:r0 + cnt_rows]

            def cnt_step(c, acc):
                kch = key_ref[r0:r0 + cnt_rows, pl.ds(pl.multiple_of(c * cnt_cols, cnt_cols), cnt_cols)]
                for q in range(cnt_cols // 128):
                    acc = acc + jnp.where(kch[:, q * 128:(q + 1) * 128] >= thr_r, 1.0, 0.0)
                return acc

            cnt_ref[r0:r0 + cnt_rows, :] = lax.fori_loop(0, nch // 2, cnt_step,
                                                         jnp.zeros((cnt_rows, 128), F32))
        cnt = jnp.dot(cnt_ref[...].astype(BF16), ones, preferred_element_type=F32)
        return jnp.where(cnt >= kk, cand, v)

    v = lax.fori_loop(0, 32, body, jnp.zeros((tq, 128), I32))
    thr = jnp.concatenate([v ^ INT_MIN] * (cw // 128), axis=1)

    def out_chunk(c, carry):
        cs = chunk_slice(c)
        o_ref[:, cs] = jnp.where(key_ref[:, cs] >= thr, 0.0, NEG_BIG).astype(o_ref.dtype)
        return carry

    lax.fori_loop(0, nch, out_chunk, 0)


def indexer_mask(proj, kit, wi, batch, seq, qi_col_block):
    t = proj.shape[0]
    tq = IDX_TILE
    nq = seq // tq
    qi_cols = IDX_HEADS * IDX_DIM
    topk = min(INDEX_TOPK_MAX, seq // 4)
    kern = functools.partial(_indexer_kernel, tq=tq, cw=ATT_TILE, topk=topk)
    return pl.pallas_call(
        kern,
        grid=(batch, nq),
        in_specs=[
            pl.BlockSpec((tq, qi_cols), lambda b, i: (b * nq + i, qi_col_block)),
            pl.BlockSpec((1, IDX_DIM, seq), lambda b, i: (b, 0, 0)),
            pl.BlockSpec((tq, IDX_HEADS), lambda b, i: (b * nq + i, 0)),
        ],
        out_specs=pl.BlockSpec((tq, seq), lambda b, i: (b * nq + i, 0)),
        out_shape=jax.ShapeDtypeStruct((t, seq), BF16),
        scratch_shapes=[pltpu.VMEM((tq, seq), I32), pltpu.VMEM((IDX_HEADS, tq, 128), F32),
                        pltpu.VMEM((tq, 128), F32)],
        compiler_params=_cparams(("parallel", "parallel")),
        name="dsa_indexer",
    )(proj, kit, wi)


def _attn_kernel(qt_ref, kt_ref, q_ref, k_ref, v_ref, mb_ref, bias_ref, o_ref, acc_ref, m_ref,
                 *, n_heads, group):
    i = qt_ref[pl.program_id(1)]
    j = kt_ref[pl.program_id(1)]
    hd = HEAD_DIM
    tq, tk = mb_ref.shape

    @pl.when(j == 0)
    def _():
        acc_ref[...] = jnp.zeros(acc_ref.shape, F32)
        m_ref[...] = jnp.full(m_ref.shape, NEG_BIG, F32)

    def heads(near):
        mb = mb_ref[...].astype(F32)
        off = i - j
        ones = jnp.ones((tk, 128), BF16)
        v_ext = [jnp.concatenate([v_ref[:, n * hd:(n + 1) * hd], ones], axis=1) for n in range(n_heads // group)]
        for h in range(n_heads):
            n = h // group
            qh = q_ref[:, h * hd:(h + 1) * hd]
            kn = k_ref[:, n * hd:(n + 1) * hd]
            s = lax.dot_general(qh, kn, (((1,), (1,)), ((), ())), preferred_element_type=F32)
            if near:
                s = s + bias_ref[off, h]
            s = s + mb
            m_prev = m_ref[h]
            m_cur = jnp.max(s, axis=1, keepdims=True)
            m_next = jnp.maximum(m_prev, m_cur)
            alpha = jnp.exp2(m_prev - m_next)
            p = jnp.exp2(s - jnp.concatenate([m_next] * (tk // 128), axis=1))
            m_ref[h] = m_next
            pv = jnp.dot(p.astype(BF16), v_ext[n], preferred_element_type=F32)
            acc_ref[h] = acc_ref[h] * jnp.concatenate([alpha, alpha], axis=1) + pv

    @pl.when(i - j < 2)
    def _():
        heads(True)

    @pl.when(i - j >= 2)
    def _():
        heads(False)

    @pl.when(j == i)
    def _():
        for h in range(n_heads):
            a = acc_ref[h]
            o_ref[:, h * hd:(h + 1) * hd] = (a[:, :hd] / a[:, hd:]).astype(o_ref.dtype)


def sparse_attention(proj, mask_bias, bias_tiles, batch, seq, n_heads):
    t = proj.shape[0]
    tq = tk = ATT_TILE
    nq = seq // tq
    q_cols = n_heads * HEAD_DIM
    kv_cols = N_KV_HEADS * HEAD_DIM
    k_blk = q_cols // kv_cols
    kern = functools.partial(_attn_kernel, n_heads=n_heads, group=n_heads // N_KV_HEADS)
    pairs = [(i, j) for i in range(nq) for j in range(i + 1)]
    q_tile = jnp.asarray([p[0] for p in pairs], I32)
    k_tile = jnp.asarray([p[1] for p in pairs], I32)
    grid_spec = pltpu.PrefetchScalarGridSpec(
        num_scalar_prefetch=2,
        grid=(batch, len(pairs)),
        in_specs=[
            pl.BlockSpec((tq, q_cols), lambda b, p, qt, kt: (b * nq + qt[p], 0)),
            pl.BlockSpec((tk, kv_cols), lambda b, p, qt, kt: (b * nq + kt[p], k_blk)),
            pl.BlockSpec((tk, kv_cols), lambda b, p, qt, kt: (b * nq + kt[p], k_blk + 1)),
            pl.BlockSpec((tq, tk), lambda b, p, qt, kt: (b * nq + qt[p], kt[p])),
            pl.BlockSpec((2, n_heads, tq, tk), lambda b, p, qt, kt: (0, 0, 0, 0)),
        ],
        out_specs=pl.BlockSpec((tq, q_cols), lambda b, p, qt, kt: (b * nq + qt[p], 0)),
        scratch_shapes=[
            pltpu.VMEM((n_heads, tq, 2 * HEAD_DIM), F32),
            pltpu.VMEM((n_heads, tq, 128), F32),
        ],
    )
    return pl.pallas_call(
        kern,
        grid_spec=grid_spec,
        out_shape=jax.ShapeDtypeStruct((t, q_cols), BF16),
        compiler_params=_cparams(("parallel", "arbitrary")),
        name="dsa_attention",
    )(q_tile, k_tile, proj, proj, proj, mask_bias, bias_tiles)


def _t5_bucket(n):
    n = jnp.maximum(n, 0)
    max_exact = NUM_BUCKETS // 2
    nf = jnp.maximum(n, 1).astype(F32)
    large = max_exact + (jnp.log(nf / max_exact) / math.log(MAX_DISTANCE / max_exact)
                         * (NUM_BUCKETS - max_exact)).astype(I32)
    large = jnp.minimum(large, NUM_BUCKETS - 1)
    return jnp.where(n < max_exact, n, large)


def _toeplitz(by_delta, n):
    h = by_delta.shape[0]
    u = jnp.concatenate([by_delta[:, :n][:, ::-1], jnp.zeros((h, 2), by_delta.dtype),
                         by_delta[:, n:][:, ::-1]], axis=1)
    rows = jnp.tile(u, (1, n))[:, :n * 2 * n].reshape(h, n, 2 * n)
    return rows[:, :, :n]


def attention_layer(x, norm_g, w_in, w_o, rel_bias, batch, seq):
    t, d = x.shape
    n_heads = w_o.shape[0] // HEAD_DIM
    q_cols = n_heads * HEAD_DIM
    kv_cols = N_KV_HEADS * HEAD_DIM
    qi_cols = IDX_HEADS * IDX_DIM
    main_cols = q_cols + 2 * kv_cols + qi_cols
    assert main_cols % qi_cols == 0 and q_cols % kv_cols == 0
    tail_cols = 128
    w_main = w_in[:, :main_cols].astype(BF16)
    w_tail = jnp.pad(w_in[:, main_cols:], ((0, 0), (0, tail_cols - (IDX_DIM + IDX_HEADS)))).astype(BF16)
    s_main = jnp.concatenate([jnp.full((q_cols,), (HEAD_DIM ** -0.5) * LOG2E, F32),
                              jnp.ones((main_cols - q_cols,), F32)]).reshape(1, main_cols)
    s_tail = jnp.concatenate([jnp.ones((IDX_DIM,), F32),
                              jnp.full((IDX_HEADS,), (IDX_HEADS ** -0.5) * (IDX_DIM ** -0.5), F32),
                              jnp.ones((tail_cols - IDX_DIM - IDX_HEADS,), F32)]).reshape(1, tail_cols)
    g2 = norm_g.reshape(1, d)
    proj = norm_matmul(x, g2, w_main, jnp.zeros((1, main_cols), F32), s_main, BF16)
    tail = norm_matmul(x, g2, w_tail, jnp.zeros((1, tail_cols), F32), s_tail, F32)
    kit = tail[:, :IDX_DIM].astype(BF16).reshape(batch, seq, IDX_DIM).transpose(0, 2, 1)
    wi = tail[:, IDX_DIM:IDX_DIM + IDX_HEADS]
    mask_bias = indexer_mask(proj, kit, wi, batch, seq, (q_cols + 2 * kv_cols) // qi_cols)

    tile = ATT_TILE
    assert tile >= MAX_DISTANCE
    dist = jnp.arange(2 * tile, dtype=I32)
    onehot = (_t5_bucket(dist)[:, None] == jnp.arange(NUM_BUCKETS, dtype=I32)).astype(F32)
    tbl = jnp.dot(onehot, rel_bias - rel_bias[NUM_BUCKETS - 1][None, :], precision=lax.Precision.HIGHEST)
    tbl = jnp.where((dist >= MAX_DISTANCE)[:, None], 0.0, tbl).T
    by_delta0 = jnp.concatenate([jnp.zeros((n_heads, tile - 1), F32), tbl[:, :tile]], axis=1)
    by_delta1 = tbl[:, 1:]
    bias_tiles = jnp.stack([_toeplitz(by_delta0, tile), _toeplitz(by_delta1, tile)], axis=0) * LOG2E

    o = sparse_attention(proj, mask_bias, bias_tiles, batch, seq, n_heads)
    return matmul_residual(o, w_o.astype(BF16), jnp.zeros((1, d), F32), x)


def conv_layer(x, norm_g, w_pw1, b_pw1, w_dw, b_dw, ln_g, ln_b, w_pw2, b_pw2, seq):
    t, d = x.shape
    u = norm_matmul_glu(x, norm_g.reshape(1, d), w_pw1.astype(BF16), b_pw1.reshape(1, 2 * d))
    fold = (CONV_FOLD, d // CONV_FOLD)
    w_dw2 = jnp.pad(w_dw[:, 0, :], ((0, CONV_HALO - CONV_WIDTH), (0, 0))).reshape(CONV_HALO, *fold)
    v = dwconv_ln_swish(u.reshape(t, *fold), w_dw2, b_dw.reshape(1, *fold), ln_g.reshape(1, *fold),
                        ln_b.reshape(1, *fold), seq)
    return matmul_residual(v.reshape(t, d).astype(BF16), w_pw2.astype(BF16), b_pw2.reshape(1, d), x)


def kernel(x, norm_mix, norm_ffn, final_norm, conv_w_pw1, conv_b_pw1, conv_w_dw, conv_b_dw, conv_ln_g,
           conv_ln_b, conv_w_pw2, conv_b_pw2, attn_w_in, attn_w_o, rel_bias, router_w, router_b, moe_w_gu,
           moe_b_gu, moe_w_down, moe_b_down):
    batch, seq, d = x.shape
    depth = norm_mix.shape[0]
    h = x.reshape(batch * seq, d)
    for i in range(depth):
        jdx = i // 2
        if i % 2 == 0:
            h = conv_layer(h, norm_mix[i], conv_w_pw1[jdx], conv_b_pw1[jdx], conv_w_dw[jdx], conv_b_dw[jdx],
                           conv_ln_g[jdx], conv_ln_b[jdx], conv_w_pw2[jdx], conv_b_pw2[jdx], seq)
        else:
            h = attention_layer(h, norm_mix[i], attn_w_in[jdx], attn_w_o[jdx], rel_bias, batch, seq)
        h = moe_layer(h, norm_ffn[i], router_w[i], router_b[i], i, moe_w_gu, moe_b_gu, moe_w_down, moe_b_down,
                      final_g=final_norm if i == depth - 1 else None)
    return h.reshape(batch, seq, d)
```

```python
import functools
import math

import numpy as np
import jax
import jax.numpy as jnp
from jax import lax
from jax.experimental import pallas as pl
from jax.experimental.pallas import tpu as pltpu

F32 = jnp.float32
BF16 = jnp.bfloat16
I32 = jnp.int32

NORM_EPS = 1e-5
CONV_WIDTH = 31
HEAD_DIM = 128
N_KV_HEADS = 4
IDX_HEADS = 16
IDX_DIM = 64
INDEX_TOPK_MAX = 256
NUM_BUCKETS = 32
MAX_DISTANCE = 128
N_EXPERTS = 32
TOP_K = 4
SWIGLU_LIMIT = 7.0
SWIGLU_ALPHA = 1.702
MOE_BLOCK = 256
MOE_ITEM_BLOCKS = 10
MOE_TFA = 512
MOE_TNB = 512
ATT_TILE = 256
IDX_TILE = 512
CONV_HALO = 32
CONV_FOLD = 8
NEG_BIG = -1e30
INT_MIN = -2147483648
LOG2E = math.log2(math.e)
VMEM_LIMIT = 56 * 1024 * 1024


def _cparams(sem):
    return pltpu.CompilerParams(dimension_semantics=sem, vmem_limit_bytes=VMEM_LIMIT)


def _rms(x, g):
    ms = jnp.mean(x * x, axis=-1, keepdims=True)
    return (x * lax.rsqrt(ms + NORM_EPS)) * g


def _sigmoid(x):
    return 1.0 / (1.0 + jnp.exp(-x))


def _norm_mm_kernel(x_ref, g_ref, w_ref, b_ref, s_ref, o_ref, hn_ref):
    @pl.when(pl.program_id(1) == 0)
    def _():
        hn_ref[...] = _rms(x_ref[...], g_ref[...]).astype(BF16)

    acc = jnp.dot(hn_ref[...], w_ref[...], preferred_element_type=F32)
    o_ref[...] = ((acc + b_ref[...]) * s_ref[...]).astype(o_ref.dtype)


def norm_matmul(x, g, w, b, s, out_dtype, tm=1024, tn=512):
    t, d = x.shape
    n = w.shape[1]
    tn = min(tn, n)
    return pl.pallas_call(
        _norm_mm_kernel,
        grid=(t // tm, n // tn),
        in_specs=[
            pl.BlockSpec((tm, d), lambda i, j: (i, 0)),
            pl.BlockSpec((1, d), lambda i, j: (0, 0)),
            pl.BlockSpec((d, tn), lambda i, j: (0, j)),
            pl.BlockSpec((1, tn), lambda i, j: (0, j)),
            pl.BlockSpec((1, tn), lambda i, j: (0, j)),
        ],
        out_specs=pl.BlockSpec((tm, tn), lambda i, j: (i, j)),
        out_shape=jax.ShapeDtypeStruct((t, n), out_dtype),
        scratch_shapes=[pltpu.VMEM((tm, d), BF16)],
        compiler_params=_cparams(("parallel", "arbitrary")),
        name="norm_matmul",
    )(x, g, w, b, s)


def _norm_mm_glu_kernel(x_ref, g_ref, wa_ref, wg_ref, ba_ref, bg_ref, o_ref, hn_ref):
    @pl.when(pl.program_id(1) == 0)
    def _():
        hn_ref[...] = _rms(x_ref[...], g_ref[...]).astype(BF16)

    hn = hn_ref[...]
    a = jnp.dot(hn, wa_ref[...], preferred_element_type=F32) + ba_ref[...]
    gt = jnp.dot(hn, wg_ref[...], preferred_element_type=F32) + bg_ref[...]
    o_ref[...] = (a * _sigmoid(gt)).astype(o_ref.dtype)


def norm_matmul_glu(x, g, w, b, tm=1024, tn=512):
    t, d = x.shape
    n = w.shape[1] // 2
    nj = n // tn
    return pl.pallas_call(
        _norm_mm_glu_kernel,
        grid=(t // tm, nj),
        in_specs=[
            pl.BlockSpec((tm, d), lambda i, j: (i, 0)),
            pl.BlockSpec((1, d), lambda i, j: (0, 0)),
            pl.BlockSpec((d, tn), lambda i, j: (0, j)),
            pl.BlockSpec((d, tn), lambda i, j: (0, j + nj)),
            pl.BlockSpec((1, tn), lambda i, j: (0, j)),
            pl.BlockSpec((1, tn), lambda i, j: (0, j + nj)),
        ],
        out_specs=pl.BlockSpec((tm, tn), lambda i, j: (i, j)),
        out_shape=jax.ShapeDtypeStruct((t, n), F32),
        scratch_shapes=[pltpu.VMEM((tm, d), BF16)],
        compiler_params=_cparams(("parallel", "arbitrary")),
        name="norm_pw1_glu",
    )(x, g, w, w, b, b)


def _mm_res_kernel(a_ref, w_ref, b_ref, r_ref, o_ref):
    acc = jnp.dot(a_ref[...], w_ref[...], preferred_element_type=F32)
    o_ref[...] = r_ref[...] + (acc + b_ref[...])


def matmul_residual(a, w, b, res, tm=1024, tn=512):
    t, k = a.shape
    n = w.shape[1]
    return pl.pallas_call(
        _mm_res_kernel,
        grid=(t // tm, n // tn),
        in_specs=[
            pl.BlockSpec((tm, k), lambda i, j: (i, 0)),
            pl.BlockSpec((k, tn), lambda i, j: (0, j)),
            pl.BlockSpec((1, tn), lambda i, j: (0, j)),
            pl.BlockSpec((tm, tn), lambda i, j: (i, j)),
        ],
        out_specs=pl.BlockSpec((tm, tn), lambda i, j: (i, j)),
        out_shape=jax.ShapeDtypeStruct((t, n), F32),
        compiler_params=_cparams(("parallel", "parallel")),
        name="matmul_residual",
    )(a, w, b, res)


def _dwconv_kernel(prev_ref, cur_ref, w_ref, bdw_ref, lg_ref, lb_ref, o_ref, buf_ref,
                   *, ts, tiles_per_seq, row_chunk, norm_chunk):
    i = pl.program_id(0)
    first = (i % tiles_per_seq) == 0
    buf_ref[0:CONV_HALO] = jnp.where(first, 0.0, prev_ref[...])
    buf_ref[CONV_HALO:] = cur_ref[...]
    _, cs, cl = cur_ref.shape
    inv_n = 1.0 / (cs * cl)
    shift = CONV_HALO - (CONV_WIDTH - 1)

    def row_body(rc, carry):
        r0 = pl.multiple_of(rc * row_chunk, row_chunk)
        parts = [jnp.broadcast_to(bdw_ref[...], (row_chunk, cs, cl)), jnp.zeros((row_chunk, cs, cl), F32)]
        for k in range(CONV_WIDTH):
            parts[k % 2] = parts[k % 2] + w_ref[k] * buf_ref[pl.ds(r0 + k + shift, row_chunk)]
        o_ref[pl.ds(r0, row_chunk)] = parts[0] + parts[1]
        return carry

    lax.fori_loop(0, ts // row_chunk, row_body, 0)

    def norm_body(rc, carry):
        rows = pl.ds(pl.multiple_of(rc * norm_chunk, norm_chunk), norm_chunk)
        y = o_ref[rows]
        mu = jnp.sum(y, axis=(1, 2), keepdims=True) * inv_n
        yc = y - mu
        var = jnp.sum(yc * yc, axis=(1, 2), keepdims=True) * inv_n
        z = (yc * lax.rsqrt(var + NORM_EPS)) * lg_ref[...] + lb_ref[...]
        o_ref[rows] = z * _sigmoid(z)
        return carry

    lax.fori_loop(0, ts // norm_chunk, norm_body, 0)


def dwconv_ln_swish(u, w_dw, b_dw, ln_g, ln_b, seq, ts=256):
    t, cs, cl = u.shape
    hb = ts // CONV_HALO
    kern = functools.partial(_dwconv_kernel, ts=ts, tiles_per_seq=seq // ts, row_chunk=8, norm_chunk=32)
    vec = pl.BlockSpec((1, cs, cl), lambda i: (0, 0, 0))
    return pl.pallas_call(
        kern,
        grid=(t // ts,),
        in_specs=[
            pl.BlockSpec((CONV_HALO, cs, cl), lambda i: (jnp.maximum(i * hb - 1, 0), 0, 0)),
            pl.BlockSpec((ts, cs, cl), lambda i: (i, 0, 0)),
            pl.BlockSpec((CONV_HALO, cs, cl), lambda i: (0, 0, 0)),
            vec, vec, vec,
        ],
        out_specs=pl.BlockSpec((ts, cs, cl), lambda i: (i, 0, 0)),
        out_shape=jax.ShapeDtypeStruct((t, cs, cl), F32),
        scratch_shapes=[pltpu.VMEM((ts + CONV_HALO, cs, cl), F32)],
        compiler_params=_cparams(("parallel",)),
        name="dwconv_ln_swish",
    )(u, u, w_dw, b_dw, ln_g, ln_b)


def _router_kernel(x_ref, g_ref, rw_ref, rb_ref, hn_ref, idx_ref, gate_ref, rank_ref, cnt_ref):
    @pl.when(pl.program_id(0) == 0)
    def _():
        cnt_ref[...] = jnp.zeros(cnt_ref.shape, F32)

    hn = _rms(x_ref[...], g_ref[...])
    hn_ref[...] = hn.astype(BF16)
    logits = jnp.dot(hn, rw_ref[...], preferred_element_type=F32,
                     precision=lax.Precision.HIGHEST) + rb_ref[...]
    tm, ne = logits.shape
    lane = lax.broadcasted_iota(I32, (tm, ne), 1).astype(F32)
    lane4 = lax.broadcasted_iota(I32, (tm, TOP_K), 1)
    work = logits
    vals, hits = [], []
    idx_out = jnp.zeros((tm, TOP_K), F32)
    for k in range(TOP_K):
        m = jnp.max(work, axis=-1, keepdims=True)
        idx = jnp.min(jnp.where(work == m, lane, float(ne)), axis=-1, keepdims=True)
        hit = lane == idx
        vals.append(m)
        hits.append(hit)
        idx_out = jnp.where(lane4 == k, idx, idx_out)
        work = jnp.where(hit, -jnp.inf, work)
    es = [jnp.exp(v - vals[0]) for v in vals]
    denom = es[0] + es[1] + es[2] + es[3]
    sel = jnp.zeros((tm, ne), F32)
    gate_out = jnp.zeros((tm, TOP_K), F32)
    for k in range(TOP_K):
        sel = jnp.where(hits[k], 1.0, sel)
        gate_out = jnp.where(lane4 == k, es[k] / denom, gate_out)
    r_i = lax.broadcasted_iota(I32, (tm, tm), 0)
    c_i = lax.broadcasted_iota(I32, (tm, tm), 1)
    tri = jnp.where(c_i < r_i, 1.0, 0.0).astype(BF16)
    rank_full = jnp.dot(tri, sel.astype(BF16), preferred_element_type=F32) + cnt_ref[...]
    rank_out = jnp.zeros((tm, TOP_K), F32)
    for k in range(TOP_K):
        rk = jnp.sum(jnp.where(hits[k], rank_full, 0.0), axis=-1, keepdims=True)
        rank_out = jnp.where(lane4 == k, rk, rank_out)
    cnt_ref[...] = cnt_ref[...] + jnp.sum(sel, axis=0, keepdims=True)
    idx_ref[...] = idx_out.astype(I32)
    gate_ref[...] = gate_out
    rank_ref[...] = rank_out.astype(I32)


def router(x, g, rw, rb, tm=512):
    t, d = x.shape
    ne = rw.shape[1]
    return pl.pallas_call(
        _router_kernel,
        grid=(t // tm,),
        in_specs=[
            pl.BlockSpec((tm, d), lambda i: (i, 0)),
            pl.BlockSpec((1, d), lambda i: (0, 0)),
            pl.BlockSpec((d, ne), lambda i: (0, 0)),
            pl.BlockSpec((1, ne), lambda i: (0, 0)),
        ],
        out_specs=[
            pl.BlockSpec((tm, d), lambda i: (i, 0)),
            pl.BlockSpec((tm, TOP_K), lambda i: (i, 0)),
            pl.BlockSpec((tm, TOP_K), lambda i: (i, 0)),
            pl.BlockSpec((tm, TOP_K), lambda i: (i, 0)),
            pl.BlockSpec((1, ne), lambda i: (0, 0)),
        ],
        out_shape=[
            jax.ShapeDtypeStruct((t, d), BF16),
            jax.ShapeDtypeStruct((t, TOP_K), I32),
            jax.ShapeDtypeStruct((t, TOP_K), F32),
            jax.ShapeDtypeStruct((t, TOP_K), I32),
            jax.ShapeDtypeStruct((1, ne), F32),
        ],
        compiler_params=_cparams(("arbitrary",)),
        name="moe_router",
    )(x, g, rw, rb)


def _moe_kernel(ie_ref, ib_ref, in_ref, x_hbm, wg_ref, wu_ref, wd_ref, bg_ref, bu_ref, bd_ref,
                o_ref, act_ref, x_ref, x_sem, *, nfa):
    w = pl.program_id(0)
    j = pl.program_id(1)
    nblk = in_ref[w]
    n_items = in_ref.shape[0]
    item_rows = x_ref.shape[0]

    def x_copy(item):
        start = pl.multiple_of(ib_ref[item] * MOE_BLOCK, MOE_BLOCK)
        return pltpu.make_async_copy(x_hbm.at[pl.ds(start, item_rows)], x_ref, x_sem)

    @pl.when((w == 0) & (j == 0))
    def _():
        x_copy(0).start()

    @pl.when(j == 0)
    def _():
        x_copy(w).wait()

    nxt = jnp.minimum(w + 1, n_items - 1)

    @pl.when((j == nfa) & (w + 1 < n_items) & (in_ref[nxt] > 0))
    def _():
        x_copy(nxt).start()

    group = 8
    group_rows = group * MOE_BLOCK
    ngroup = nblk // group

    def for_row_groups(fn):
        def body(p, carry):
            fn(pl.multiple_of(p * group_rows, group_rows), group_rows)
            return carry

        lax.fori_loop(0, ngroup, body, 0)
        rem = nblk % group
        start = ngroup * group
        for part in (4, 2, 1):
            @pl.when((rem & part) != 0)
            def _():
                done = start + (rem & ~(2 * part - 1))
                fn(pl.multiple_of(done * MOE_BLOCK, part * MOE_BLOCK), part * MOE_BLOCK)

    @pl.when((nblk > 0) & (j < nfa))
    def _():
        wg = wg_ref[0, 0].astype(BF16)
        wu = wu_ref[0, 0].astype(BF16)
        bg = bg_ref[0, 0]
        bu = bu_ref[0, 0]
        cols = pl.ds(pl.multiple_of(j * MOE_TFA, MOE_TFA), MOE_TFA)

        def gate_up(r0, nrows):
            for s0 in range(0, nrows, MOE_BLOCK):
                rows = pl.ds(r0 + s0, MOE_BLOCK)
                xr = x_ref[rows, :]
                g = jnp.dot(xr, wg, preferred_element_type=F32) + bg
                u = jnp.dot(xr, wu, preferred_element_type=F32) + bu
                g = jnp.minimum(g, SWIGLU_LIMIT)
                u = jnp.clip(u, -SWIGLU_LIMIT, SWIGLU_LIMIT)
                act = (u + 1.0) * (g * _sigmoid(g * SWIGLU_ALPHA))
                act_ref[rows, cols] = act.astype(BF16)

        for_row_groups(gate_up)

    @pl.when((nblk > 0) & (j >= nfa))
    def _():
        wd = wd_ref[0, 0].astype(BF16)
        bd = bd_ref[0, 0]

        def down(r0, nrows):
            for s0 in range(0, nrows, MOE_BLOCK):
                rows = pl.ds(r0 + s0, MOE_BLOCK)
                y = jnp.dot(act_ref[rows, :], wd, preferred_element_type=F32) + bd
                o_ref[rows, :] = y.astype(o_ref.dtype)

        for_row_groups(down)


def moe_experts(xs, item_expert, item_blk0, item_nblk, n_used, layer, w_gu, b_gu, w_down, b_down):
    p_rows, d = xs.shape
    depth, ne, _, dff2 = w_gu.shape
    dff = dff2 // 2
    nfa = dff // MOE_TFA
    nfb = d // MOE_TNB
    n_items = item_expert.shape[0]
    item_rows = MOE_ITEM_BLOCKS * MOE_BLOCK

    def ja(w, j, inb):
        return jnp.where(inb[w] > 0, jnp.minimum(j, nfa - 1), nfa - 1)

    def jb(w, j, inb):
        return jnp.where(inb[w] > 0, jnp.maximum(j - nfa, 0), nfb - 1)

    b_gu4 = b_gu.reshape(depth, ne, 1, dff2)
    b_down4 = b_down.reshape(depth, ne, 1, d)
    grid_spec = pltpu.PrefetchScalarGridSpec(
        num_scalar_prefetch=3,
        grid=(n_used, nfa + nfb),
        in_specs=[
            pl.BlockSpec(memory_space=pl.ANY),
            pl.BlockSpec((1, 1, d, MOE_TFA), lambda w, j, ie, ib, inb: (layer, ie[w], 0, ja(w, j, inb))),
            pl.BlockSpec((1, 1, d, MOE_TFA), lambda w, j, ie, ib, inb: (layer, ie[w], 0, nfa + ja(w, j, inb))),
            pl.BlockSpec((1, 1, dff, MOE_TNB), lambda w, j, ie, ib, inb: (layer, ie[w], 0, jb(w, j, inb))),
            pl.BlockSpec((1, 1, 1, MOE_TFA), lambda w, j, ie, ib, inb: (layer, ie[w], 0, ja(w, j, inb))),
            pl.BlockSpec((1, 1, 1, MOE_TFA), lambda w, j, ie, ib, inb: (layer, ie[w], 0, nfa + ja(w, j, inb))),
            pl.BlockSpec((1, 1, 1, MOE_TNB), lambda w, j, ie, ib, inb: (layer, ie[w], 0, jb(w, j, inb))),
        ],
        out_specs=pl.BlockSpec((item_rows, MOE_TNB), lambda w, j, ie, ib, inb: (w, jb(w, j, inb))),
        scratch_shapes=[pltpu.VMEM((item_rows, dff), BF16), pltpu.VMEM((item_rows, d), BF16),
                        pltpu.SemaphoreType.DMA(())],
    )
    return pl.pallas_call(
        functools.partial(_moe_kernel, nfa=nfa),
        grid_spec=grid_spec,
        out_shape=jax.ShapeDtypeStruct((n_items * item_rows, d), BF16),
        compiler_params=_cparams(("arbitrary", "arbitrary")),
        name="moe_experts",
    )(item_expert, item_blk0, item_nblk, xs, w_gu, w_gu, w_down, b_gu4, b_gu4, b_down4)


def _combine_kernel(*refs, final):
    y_refs = refs[:TOP_K]
    g_ref, x_ref = refs[TOP_K], refs[TOP_K + 1]
    o_ref = refs[-1]
    g = g_ref[...]
    acc = x_ref[...]
    for k in range(TOP_K):
        acc = acc + g[:, k:k + 1] * y_refs[k][...].astype(F32)
    if final:
        acc = _rms(acc, refs[TOP_K + 2][...])
    o_ref[...] = acc


def moe_combine(yk, gate4, x, final_g=None, tm=256):
    t, d = x.shape
    final = final_g is not None
    nt = t // tm

    def y_map(k):
        return lambda i: (k * nt + i, 0)

    in_specs = [pl.BlockSpec((tm, d), y_map(k)) for k in range(TOP_K)] + [
        pl.BlockSpec((tm, TOP_K), lambda i: (i, 0)),
        pl.BlockSpec((tm, d), lambda i: (i, 0)),
    ]
    args = [yk] * TOP_K + [gate4, x]
    if final:
        in_specs.append(pl.BlockSpec((1, d), lambda i: (0, 0)))
        args.append(final_g.reshape(1, d))
    return pl.pallas_call(
        functools.partial(_combine_kernel, final=final),
        grid=(t // tm,),
        in_specs=in_specs,
        out_specs=pl.BlockSpec((tm, d), lambda i: (i, 0)),
        out_shape=jax.ShapeDtypeStruct((t, d), F32),
        compiler_params=_cparams(("parallel",)),
        name="moe_combine",
    )(*args)


def _lookup(table, idx):
    onehot = idx[..., None] == jnp.arange(table.shape[0], dtype=I32)
    return jnp.sum(jnp.where(onehot, table, 0), axis=-1)


def moe_layer(x, norm_g, router_w, router_b, layer, w_gu, b_gu, w_down, b_down, final_g=None):
    t, d = x.shape
    ne = router_w.shape[1]
    hn, top_idx, gate4, rank4, cnt = router(x, norm_g.reshape(1, d), router_w, router_b.reshape(1, ne))

    n_assign = t * TOP_K
    n_blocks = -(-n_assign // MOE_BLOCK) + ne
    p_rows = n_blocks * MOE_BLOCK
    item_rows = MOE_ITEM_BLOCKS * MOE_BLOCK
    counts = cnt[0].astype(I32)
    nb = (counts + MOE_BLOCK - 1) // MOE_BLOCK
    blk_start = jnp.cumsum(nb) - nb

    max_items = n_blocks // MOE_ITEM_BLOCKS + ne
    n_it = (nb + MOE_ITEM_BLOCKS - 1) // MOE_ITEM_BLOCKS
    it_cum = jnp.cumsum(n_it)
    it_start = it_cum - n_it
    total_items = it_cum[-1]
    base_e = nb // jnp.maximum(n_it, 1)
    rem_e = nb % jnp.maximum(n_it, 1)
    wids = jnp.arange(max_items, dtype=I32)
    e_of = jnp.minimum(jnp.searchsorted(it_cum, wids, side="right"), ne - 1).astype(I32)
    local = wids - it_start[e_of]
    size = base_e[e_of] + (local < rem_e[e_of]).astype(I32)
    off = local * base_e[e_of] + jnp.minimum(local, rem_e[e_of])
    valid = wids < total_items
    last_e = e_of[jnp.maximum(total_items - 1, 0)]
    item_expert = jnp.where(valid, e_of, last_e).astype(I32)
    item_nblk = jnp.where(valid, size, 0).astype(I32)
    item_blk0 = jnp.where(valid, blk_start[e_of] + off, 0).astype(I32)

    a_e = top_idx.T.reshape(-1)
    a_rank = rank4.T.reshape(-1)
    blk_in_e = a_rank // MOE_BLOCK
    within = a_rank % MOE_BLOCK
    a_base = _lookup(base_e, a_e)
    a_rem = _lookup(rem_e, a_e)
    dest = (_lookup(blk_start, a_e) + blk_in_e) * MOE_BLOCK + within
    big = a_base + 1
    n_big = a_rem * big
    in_big = blk_in_e < n_big
    num = jnp.where(in_big, blk_in_e, blk_in_e - n_big)
    den = jnp.where(in_big, big, jnp.maximum(a_base, 1))
    quo = jnp.floor((num.astype(F32) + 0.5) / den.astype(F32)).astype(I32)
    a_local = jnp.where(in_big, 0, a_rem) + quo
    a_slot = num - quo * den
    out_row = ((_lookup(it_start, a_e) + a_local) * MOE_ITEM_BLOCKS + a_slot) * MOE_BLOCK + within

    n_rows = p_rows + item_rows
    row_token = (jnp.arange(n_rows, dtype=I32) % t).at[dest].set(jnp.tile(jnp.arange(t, dtype=I32), TOP_K))
    xs = hn[row_token]
    ys = moe_experts(xs, item_expert, item_blk0, item_nblk, total_items.astype(I32), layer, w_gu, b_gu,
                     w_down, b_down)
    yk = ys[out_row]
    return moe_combine(yk, gate4, x, final_g)


def _indexer_kernel(qi_ref, kit_ref, wi_ref, o_ref, key_ref, wb_ref, cnt_ref, *, tq, cw, topk):
    i = pl.program_id(1)
    assert (tq // cw) % 2 == 0
    nch = (i + 1) * (tq // cw)
    o_ref[...] = jnp.full(o_ref.shape, NEG_BIG, o_ref.dtype)
    col_minus_row = lax.broadcasted_iota(I32, (tq, cw), 1) - lax.broadcasted_iota(I32, (tq, cw), 0)
    wi = wi_ref[...]
    for h in range(IDX_HEADS):
        wb_ref[h] = jnp.broadcast_to(wi[:, h:h + 1], (tq, 128))

    def chunk_slice(c):
        return pl.ds(pl.multiple_of(c * cw, cw), cw)

    def score_chunk(c, carry):
        cs = chunk_slice(c)
        kc = kit_ref[0, :, cs]
        acc = jnp.zeros((tq, cw), F32)
        for h in range(IDX_HEADS):
            s = jnp.dot(qi_ref[:, h * IDX_DIM:(h + 1) * IDX_DIM], kc, preferred_element_type=F32)
            wh = wb_ref[h]
            acc = acc + jnp.concatenate([wh] * (cw // 128), axis=1) * jnp.maximum(s, 0.0)
        bits = pltpu.bitcast(acc, I32)
        key = bits ^ ((bits >> 31) & 0x7FFFFFFF)
        key = jnp.where(col_minus_row <= i * tq - c * cw, key, INT_MIN)
        key_ref[:, cs] = key
        return carry

    lax.fori_loop(0, nch, score_chunk, 0)

    pos = i * tq + lax.broadcasted_iota(I32, (tq, 128), 0)
    kk = jnp.minimum(pos + 1, topk).astype(F32)

    cnt_rows = 64
    cnt_cols = 2 * cw

    ones = jnp.ones((128, 128), BF16)

    def body(it, v):
        cand = v | jnp.left_shift(jnp.int32(1), 31 - it)
        thr = cand ^ INT_MIN
        for r0 in range(0, tq, cnt_rows):
            thr_r = thr[r0:r0 + cnt_rows]

            def cnt_step(c, acc):
                kch = key_ref[r0:r0 + cnt_rows, pl.ds(pl.multiple_of(c * cnt_cols, cnt_cols), cnt_cols)]
                for q in range(cnt_cols // 128):
                    acc = acc + jnp.where(kch[:, q * 128:(q + 1) * 128] >= thr_r, 1.0, 0.0)
                return acc

            cnt_ref[r0:r0 + cnt_rows, :] = lax.fori_loop(0, nch // 2, cnt_step,
                                                         jnp.zeros((cnt_rows, 128), F32))
        cnt = jnp.dot(cnt_ref[...].astype(BF16), ones, preferred_element_type=F32)
        return jnp.where(cnt >= kk, cand, v)

    v = lax.fori_loop(0, 32, body, jnp.zeros((tq, 128), I32))
    thr = jnp.concatenate([v ^ INT_MIN] * (cw // 128), axis=1)

    def out_chunk(c, carry):
        cs = chunk_slice(c)
        o_ref[:, cs] = jnp.where(key_ref[:, cs] >= thr, 0.0, NEG_BIG).astype(o_ref.dtype)
        return carry

    lax.fori_loop(0, nch, out_chunk, 0)


def indexer_mask(proj, kit, wi, batch, seq, qi_col_block):
    t = proj.shape[0]
    tq = IDX_TILE
    nq = seq // tq
    qi_cols = IDX_HEADS * IDX_DIM
    topk = min(INDEX_TOPK_MAX, seq // 4)
    kern = functools.partial(_indexer_kernel, tq=tq, cw=ATT_TILE, topk=topk)
    return pl.pallas_call(
        kern,
        grid=(batch, nq),
        in_specs=[
            pl.BlockSpec((tq, qi_cols), lambda b, i: (b * nq + i, qi_col_block)),
            pl.BlockSpec((1, IDX_DIM, seq), lambda b, i: (b, 0, 0)),
            pl.BlockSpec((tq, IDX_HEADS), lambda b, i: (b * nq + i, 0)),
        ],
        out_specs=pl.BlockSpec((tq, seq), lambda b, i: (b * nq + i, 0)),
        out_shape=jax.ShapeDtypeStruct((t, seq), BF16),
        scratch_shapes=[pltpu.VMEM((tq, seq), I32), pltpu.VMEM((IDX_HEADS, tq, 128), F32),
                        pltpu.VMEM((tq, 128), F32)],
        compiler_params=_cparams(("parallel", "parallel")),
        name="dsa_indexer",
    )(proj, kit, wi)


def _attn_kernel(qt_ref, kt_ref, q_ref, k_ref, v_ref, mb_ref, bias_ref, o_ref, acc_ref, m_ref,
                 *, n_heads, group):
    i = qt_ref[pl.program_id(1)]
    j = kt_ref[pl.program_id(1)]
    hd = HEAD_DIM
    tq, tk = mb_ref.shape

    @pl.when(j == 0)
    def _():
        acc_ref[...] = jnp.zeros(acc_ref.shape, F32)
        m_ref[...] = jnp.full(m_ref.shape, NEG_BIG, F32)

    def heads(near):
        mb = mb_ref[...].astype(F32)
        off = i - j
        ones = jnp.ones((tk, 128), BF16)
        v_ext = [jnp.concatenate([v_ref[:, n * hd:(n + 1) * hd], ones], axis=1) for n in range(n_heads // group)]
        for h in range(n_heads):
            n = h // group
            qh = q_ref[:, h * hd:(h + 1) * hd]
            kn = k_ref[:, n * hd:(n + 1) * hd]
            s = lax.dot_general(qh, kn, (((1,), (1,)), ((), ())), preferred_element_type=F32)
            if near:
                s = s + bias_ref[off, h]
            s = s + mb
            m_prev = m_ref[h]
            m_cur = jnp.max(s, axis=1, keepdims=True)
            m_next = jnp.maximum(m_prev, m_cur)
            alpha = jnp.exp2(m_prev - m_next)
            p = jnp.exp2(s - jnp.concatenate([m_next] * (tk // 128), axis=1))
            m_ref[h] = m_next
            pv = jnp.dot(p.astype(BF16), v_ext[n], preferred_element_type=F32)
            acc_ref[h] = acc_ref[h] * jnp.concatenate([alpha, alpha], axis=1) + pv

    @pl.when(i - j < 2)
    def _():
        heads(True)

    @pl.when(i - j >= 2)
    def _():
        heads(False)

    @pl.when(j == i)
    def _():
        for h in range(n_heads):
            a = acc_ref[h]
            o_ref[:, h * hd:(h + 1) * hd] = (a[:, :hd] / a[:, hd:]).astype(o_ref.dtype)


def sparse_attention(proj, mask_bias, bias_tiles, batch, seq, n_heads):
    t = proj.shape[0]
    tq = tk = ATT_TILE
    nq = seq // tq
    q_cols = n_heads * HEAD_DIM
    kv_cols = N_KV_HEADS * HEAD_DIM
    k_blk = q_cols // kv_cols
    kern = functools.partial(_attn_kernel, n_heads=n_heads, group=n_heads // N_KV_HEADS)
    pairs = [(i, j) for i in range(nq) for j in range(i + 1)]
    q_tile = jnp.asarray([p[0] for p in pairs], I32)
    k_tile = jnp.asarray([p[1] for p in pairs], I32)
    grid_spec = pltpu.PrefetchScalarGridSpec(
        num_scalar_prefetch=2,
        grid=(batch, len(pairs)),
        in_specs=[
            pl.BlockSpec((tq, q_cols), lambda b, p, qt, kt: (b * nq + qt[p], 0)),
            pl.BlockSpec((tk, kv_cols), lambda b, p, qt, kt: (b * nq + kt[p], k_blk)),
            pl.BlockSpec((tk, kv_cols), lambda b, p, qt, kt: (b * nq + kt[p], k_blk + 1)),
            pl.BlockSpec((tq, tk), lambda b, p, qt, kt: (b * nq + qt[p], kt[p])),
            pl.BlockSpec((2, n_heads, tq, tk), lambda b, p, qt, kt: (0, 0, 0, 0)),
        ],
        out_specs=pl.BlockSpec((tq, q_cols), lambda b, p, qt, kt: (b * nq + qt[p], 0)),
        scratch_shapes=[
            pltpu.VMEM((n_heads, tq, 2 * HEAD_DIM), F32),
            pltpu.VMEM((n_heads, tq, 128), F32),
        ],
    )
    return pl.pallas_call(
        kern,
        grid_spec=grid_spec,
        out_shape=jax.ShapeDtypeStruct((t, q_cols), BF16),
        compiler_params=_cparams(("parallel", "arbitrary")),
        name="dsa_attention",
    )(q_tile, k_tile, proj, proj, proj, mask_bias, bias_tiles)


def _t5_bucket(n):
    n = jnp.maximum(n, 0)
    max_exact = NUM_BUCKETS // 2
    nf = jnp.maximum(n, 1).astype(F32)
    large = max_exact + (jnp.log(nf / max_exact) / math.log(MAX_DISTANCE / max_exact)
                         * (NUM_BUCKETS - max_exact)).astype(I32)
    large = jnp.minimum(large, NUM_BUCKETS - 1)
    return jnp.where(n < max_exact, n, large)


def _toeplitz(by_delta, n):
    h = by_delta.shape[0]
    u = jnp.concatenate([by_delta[:, :n][:, ::-1], jnp.zeros((h, 2), by_delta.dtype),
                         by_delta[:, n:][:, ::-1]], axis=1)
    rows = jnp.tile(u, (1, n))[:, :n * 2 * n].reshape(h, n, 2 * n)
    return rows[:, :, :n]


def attention_layer(x, norm_g, w_in, w_o, rel_bias, batch, seq):
    t, d = x.shape
    n_heads = w_o.shape[0] // HEAD_DIM
    q_cols = n_heads * HEAD_DIM
    kv_cols = N_KV_HEADS * HEAD_DIM
    qi_cols = IDX_HEADS * IDX_DIM
    main_cols = q_cols + 2 * kv_cols + qi_cols
    assert main_cols % qi_cols == 0 and q_cols % kv_cols == 0
    tail_cols = 128
    w_main = w_in[:, :main_cols].astype(BF16)
    w_tail = jnp.pad(w_in[:, main_cols:], ((0, 0), (0, tail_cols - (IDX_DIM + IDX_HEADS)))).astype(BF16)
    s_main = jnp.concatenate([jnp.full((q_cols,), (HEAD_DIM ** -0.5) * LOG2E, F32),
                              jnp.ones((main_cols - q_cols,), F32)]).reshape(1, main_cols)
    s_tail = jnp.concatenate([jnp.ones((IDX_DIM,), F32),
                              jnp.full((IDX_HEADS,), (IDX_HEADS ** -0.5) * (IDX_DIM ** -0.5), F32),
                              jnp.ones((tail_cols - IDX_DIM - IDX_HEADS,), F32)]).reshape(1, tail_cols)
    g2 = norm_g.reshape(1, d)
    proj = norm_matmul(x, g2, w_main, jnp.zeros((1, main_cols), F32), s_main, BF16)
    tail = norm_matmul(x, g2, w_tail, jnp.zeros((1, tail_cols), F32), s_tail, F32)
    kit = tail[:, :IDX_DIM].astype(BF16).reshape(batch, seq, IDX_DIM).transpose(0, 2, 1)
    wi = tail[:, IDX_DIM:IDX_DIM + IDX_HEADS]
    mask_bias = indexer_mask(proj, kit, wi, batch, seq, (q_cols + 2 * kv_cols) // qi_cols)

    tile = ATT_TILE
    assert tile >= MAX_DISTANCE
    dist = jnp.arange(2 * tile, dtype=I32)
    onehot = (_t5_bucket(dist)[:, None] == jnp.arange(NUM_BUCKETS, dtype=I32)).astype(F32)
    tbl = jnp.dot(onehot, rel_bias - rel_bias[NUM_BUCKETS - 1][None, :], precision=lax.Precision.HIGHEST)
    tbl = jnp.where((dist >= MAX_DISTANCE)[:, None], 0.0, tbl).T
    by_delta0 = jnp.concatenate([jnp.zeros((n_heads, tile - 1), F32), tbl[:, :tile]], axis=1)
    by_delta1 = tbl[:, 1:]
    bias_tiles = jnp.stack([_toeplitz(by_delta0, tile), _toeplitz(by_delta1, tile)], axis=0) * LOG2E

    o = sparse_attention(proj, mask_bias, bias_tiles, batch, seq, n_heads)
    return matmul_residual(o, w_o.astype(BF16), jnp.zeros((1, d), F32), x)


def conv_layer(x, norm_g, w_pw1, b_pw1, w_dw, b_dw, ln_g, ln_b, w_pw2, b_pw2, seq):
    t, d = x.shape
    u = norm_matmul_glu(x, norm_g.reshape(1, d), w_pw1.astype(BF16), b_pw1.reshape(1, 2 * d))
    fold = (CONV_FOLD, d // CONV_FOLD)
    w_dw2 = jnp.pad(w_dw[:, 0, :], ((0, CONV_HALO - CONV_WIDTH), (0, 0))).reshape(CONV_HALO, *fold)
    v = dwconv_ln_swish(u.reshape(t, *fold), w_dw2, b_dw.reshape(1, *fold), ln_g.reshape(1, *fold),
                        ln_b.reshape(1, *fold), seq)
    return matmul_residual(v.reshape(t, d).astype(BF16), w_pw2.astype(BF16), b_pw2.reshape(1, d), x)


def kernel(x, norm_mix, norm_ffn, final_norm, conv_w_pw1, conv_b_pw1, conv_w_dw, conv_b_dw, conv_ln_g,
           conv_ln_b, conv_w_pw2, conv_b_pw2, attn_w_in, attn_w_o, rel_bias, router_w, router_b, moe_w_gu,
           moe_b_gu, moe_w_down, moe_b_down):
    batch, seq, d = x.shape
    depth = norm_mix.shape[0]
    h = x.reshape(batch * seq, d)
    for i in range(depth):
        jdx = i // 2
        if i % 2 == 0:
            h = conv_layer(h, norm_mix[i], conv_w_pw1[jdx], conv_b_pw1[jdx], conv_w_dw[jdx], conv_b_dw[jdx],
                           conv_ln_g[jdx], conv_ln_b[jdx], conv_w_pw2[jdx], conv_b_pw2[jdx], seq)
        else:
            h = attention_layer(h, norm_mix[i], attn_w_in[jdx], attn_w_o[jdx], rel_bias, batch, seq)
        h = moe_layer(h, norm_ffn[i], router_w[i], router_b[i], i, moe_w_gu, moe_b_gu, moe_w_down, moe_b_down,
                      final_g=final_norm if i == depth - 1 else None)
    return h.reshape(batch, seq, d)
```
